```python
import jax, jax.numpy as jnp
from jax import lax
import numpy as np

D_MODEL = 1024
BATCH = 4
SEQ = 8192
DEPTH = 4
DEC_BATCH = 8
DEC_SEQ = 16
PAST_LEN = 2048

CHUNK = 64
Q_BLOCK = 128
ROPE_THETA = 500000.0
RMS_EPS = 1e-6
A_HEADS = 8
A_Q_LORA = 256
A_KV_LORA = 128
A_NOPE = 64
A_ROPE = 32
A_VDIM = 64
A_SCALE = (A_NOPE + A_ROPE) ** -0.5
B_HEADS = 8
B_HEAD_DIM = 64
B_ROT = B_HEAD_DIM // 4
B_SCALE = B_HEAD_DIM ** -0.5
B_TOPK_MAX = 256
IDX_HEADS = 8
IDX_DIM = 64
IDX_ROT = IDX_DIM // 4
C_HEADS = 8
C_HEAD_DIM = 64
C_SCALE = C_HEAD_DIM ** -0.5
C_LEFT_CHUNKS = 8
C_REACH = C_LEFT_CHUNKS * CHUNK
C_REL_CLIP = 128
N_BRANCH = 3
D_FF = 4 * D_MODEL
A_W = A_HEADS * A_VDIM
B_W = B_HEADS * B_HEAD_DIM
C_W = C_HEADS * C_HEAD_DIM
IN_SPLITS = (A_Q_LORA, A_KV_LORA, A_ROPE, B_W, B_W, B_W, IDX_HEADS * IDX_DIM, IDX_DIM, IDX_HEADS, C_W, C_W, C_W, N_BRANCH * D_MODEL)
IN_DIM = sum(IN_SPLITS)

kernel_name = 'hybrid_streaming_mla_dsa_band_step'


def rmsnorm(x, g):
    xf = x.astype(jnp.float32)
    y = xf * lax.rsqrt(jnp.mean(xf * xf, axis=-1, keepdims=True) + RMS_EPS)
    return (y * g.astype(jnp.float32)).astype(x.dtype)


def split_cols(z):
    offsets = np.cumsum(IN_SPLITS)[:-1].tolist()
    return jnp.split(z, offsets, axis=-1)


def rope(x, pos, rot):
    half = rot // 2
    inv_freq = ROPE_THETA ** (-jnp.arange(half, dtype=jnp.float32) / half)
    ang = pos.astype(jnp.float32)[:, None] * inv_freq[None, :]
    ang = ang.reshape(ang.shape[0], *([1] * (x.ndim - 3)), half)
    cos, sin = jnp.cos(ang), jnp.sin(ang)
    x1 = x[..., :half].astype(jnp.float32)
    x2 = x[..., half:rot].astype(jnp.float32)
    rotated = jnp.concatenate([x1 * cos - x2 * sin, x2 * cos + x1 * sin], axis=-1).astype(x.dtype)
    return jnp.concatenate([rotated, x[..., rot:]], axis=-1)


def rel_index(d):
    return jnp.clip(d, -C_REL_CLIP, C_REL_CLIP) + C_REL_CLIP


def chunk_visible(q_pos, k_pos):
    return (k_pos[None, :] // CHUNK) <= (q_pos[:, None] // CHUNK)


def over_query_blocks(fn, q_arrays, q_pos):
    n_blk = q_pos.shape[0] // Q_BLOCK
    def blocks(a):
        return jnp.moveaxis(a.reshape(a.shape[0], n_blk, Q_BLOCK, *a.shape[2:]), 1, 0)
    xs = tuple(blocks(a) for a in q_arrays) + (q_pos.reshape(n_blk, Q_BLOCK),)
    out = jnp.moveaxis(lax.map(lambda args: fn(*args), xs), 0, 1)
    return out.reshape(out.shape[0], n_blk * Q_BLOCK, *out.shape[3:])


def mla_attend(q_nope, q_rope, q_pos, k_nope, k_rope, v, k_pos):
    s = (jnp.einsum('bqhd,bshd->bhqs', q_nope, k_nope)
         + jnp.einsum('bqhd,bsd->bhqs', q_rope, k_rope)).astype(jnp.float32) * A_SCALE
    s = jnp.where(chunk_visible(q_pos, k_pos), s, -jnp.inf)
    p = jax.nn.softmax(s, axis=-1).astype(v.dtype)
    return jnp.einsum('bhqs,bshd->bqhd', p, v)


def dsa_attend(q, qi, wi, q_pos, k, v, ki, k_pos, topk):
    act = jax.nn.relu(jnp.einsum('bqhd,bsd->bqhs', qi, ki).astype(jnp.float32))
    score = jnp.einsum('bqhs,bqh->bqs', act, wi.astype(jnp.float32))
    vis = jnp.broadcast_to(chunk_visible(q_pos, k_pos), score.shape)
    _, sel = lax.top_k(jnp.where(vis, score, -jnp.inf), topk)
    valid = jnp.take_along_axis(vis, sel, axis=-1)
    gather = jax.vmap(lambda rows, idx: rows[idx])
    k_sel, v_sel = gather(k, sel), gather(v, sel)
    s = jnp.einsum('bqhd,bqkhd->bhqk', q, k_sel).astype(jnp.float32) * B_SCALE
    s = jnp.where(valid[:, None], s, -jnp.inf)
    p = jax.nn.softmax(s, axis=-1).astype(v.dtype)
    return jnp.einsum('bhqk,bqkhd->bqhd', p, v_sel)


def band_attend_prompt(q, k, v, rel_bias):
    b_, s_ = q.shape[:2]
    n_c = s_ // CHUNK
    n_band = C_LEFT_CHUNKS + 1
    band = jnp.arange(n_c)[:, None] + jnp.arange(n_band)[None, :]
    def banded(a):
        ap = jnp.pad(a, ((0, 0), (C_REACH, 0), (0, 0), (0, 0)))
        ap = ap.reshape(b_, n_c + C_LEFT_CHUNKS, CHUNK, C_HEADS, C_HEAD_DIM)
        return ap[:, band].reshape(b_, n_c, n_band * CHUNK, C_HEADS, C_HEAD_DIM)
    kb, vb = banded(k), banded(v)
    off = jnp.arange(CHUNK)
    k_off = ((jnp.arange(n_band)[:, None] - C_LEFT_CHUNKS) * CHUNK + off[None, :]).reshape(-1)
    bias = rel_bias[:, rel_index(off[:, None] - k_off[None, :])].astype(jnp.float32)
    k_pos = jnp.arange(n_c)[:, None] * CHUNK + k_off[None, :]
    qc = q.reshape(b_, n_c, CHUNK, C_HEADS, C_HEAD_DIM)
    s = jnp.einsum('bcqhd,bckhd->bchqk', qc, kb).astype(jnp.float32) * C_SCALE + bias[None, None]
    s = jnp.where((k_pos >= 0)[None, :, None, None, :], s, -jnp.inf)
    p = jax.nn.softmax(s, axis=-1).astype(v.dtype)
    return jnp.einsum('bchqk,bckhd->bcqhd', p, vb).reshape(b_, s_, C_HEADS, C_HEAD_DIM)


def band_attend_step(q, q_pos, k, v, k_pos, rel_bias):
    q_c, k_c = q_pos // CHUNK, k_pos // CHUNK
    vis = (k_c[None, :] <= q_c[:, None]) & (k_c[None, :] >= q_c[:, None] - C_LEFT_CHUNKS)
    bias = rel_bias[:, rel_index(q_pos[:, None] - k_pos[None, :])].astype(jnp.float32)
    s = jnp.einsum('bqhd,bshd->bhqs', q, k).astype(jnp.float32) * C_SCALE + bias[None]
    s = jnp.where(vis[None, None], s, -jnp.inf)
    p = jax.nn.softmax(s, axis=-1).astype(v.dtype)
    return jnp.einsum('bhqs,bshd->bqhd', p, v)


def token_mixers(h, pos, past, w_in, a_q_norm, a_kv_norm, a_w_uq, a_w_uk, a_w_uv, c_rel_bias, w_oa, w_ob, w_oc, w_out):
    b_, s_ = h.shape[:2]
    (a_q, a_kv, a_kr, b_q, b_k, b_v, i_q, i_k, i_w, c_q, c_k, c_v, gates) = split_cols(h @ w_in)
    heads = lambda a, n, d: a.reshape(b_, s_, n, d)
    q_a = heads(rmsnorm(a_q, a_q_norm) @ a_w_uq, A_HEADS, A_NOPE + A_ROPE)
    a_qn, a_qr = q_a[..., :A_NOPE], rope(q_a[..., A_NOPE:], pos, A_ROPE)
    a_ckv = rmsnorm(a_kv, a_kv_norm)
    a_kr = rope(a_kr, pos, A_ROPE)
    b_q = rope(heads(b_q, B_HEADS, B_HEAD_DIM), pos, B_ROT)
    b_k = rope(heads(b_k, B_HEADS, B_HEAD_DIM), pos, B_ROT)
    b_v = heads(b_v, B_HEADS, B_HEAD_DIM)
    i_q = rope(heads(i_q, IDX_HEADS, IDX_DIM), pos, IDX_ROT) * IDX_DIM ** -0.5
    i_k = rope(i_k, pos, IDX_ROT)
    i_w = i_w * IDX_HEADS ** -0.5
    c_q, c_k, c_v = (heads(a, C_HEADS, C_HEAD_DIM) for a in (c_q, c_k, c_v))

    if past is None:
        kv_ckv, kv_kr, kv_bk, kv_bv, kv_ik = a_ckv, a_kr, b_k, b_v, i_k
        k_pos = pos
    else:
        p_ckv, p_kr, p_bk, p_bv, p_ik, p_ck, p_cv = past
        cat = lambda old, new: jnp.concatenate([old.astype(new.dtype), new], axis=1)
        kv_ckv, kv_kr = cat(p_ckv, a_ckv), cat(p_kr, a_kr)
        kv_bk, kv_bv, kv_ik = cat(p_bk, b_k), cat(p_bv, b_v), cat(p_ik, i_k)
        past_len = p_ckv.shape[1]
        k_pos = jnp.arange(past_len + s_, dtype=jnp.int32)
    k_nope = (kv_ckv @ a_w_uk).reshape(b_, -1, A_HEADS, A_NOPE)
    v_a = (kv_ckv @ a_w_uv).reshape(b_, -1, A_HEADS, A_VDIM)
    topk = min(B_TOPK_MAX, k_pos.shape[0] // 4)

    def attend_a(qn, qr, qp):
        return mla_attend(qn, qr, qp, k_nope, kv_kr, v_a, k_pos)

    def attend_b(qb, qi, wi, qp):
        return dsa_attend(qb, qi, wi, qp, kv_bk, kv_bv, kv_ik, k_pos, topk)

    if past is None:
        o_a = over_query_blocks(attend_a, (a_qn, a_qr), pos)
        o_b = over_query_blocks(attend_b, (b_q, i_q, i_w), pos)
        o_c = band_attend_prompt(c_q, c_k, c_v, c_rel_bias)
        keep = min(C_REACH, s_)
        new_rows = (a_ckv, a_kr, b_k, b_v, i_k, c_k[:, s_ - keep:], c_v[:, s_ - keep:])
    else:
        o_a = attend_a(a_qn, a_qr, pos)
        o_b = attend_b(b_q, i_q, i_w, pos)
        w_c = p_ck.shape[1]
        c_pos = jnp.arange(past_len - w_c, past_len + s_, dtype=jnp.int32)
        o_c = band_attend_step(c_q, pos, cat(p_ck, c_k), cat(p_cv, c_v), c_pos, c_rel_bias)
        new_rows = (a_ckv, a_kr, b_k, b_v, i_k, c_k, c_v)

    g = jax.nn.sigmoid(gates.astype(jnp.float32)).astype(h.dtype).reshape(b_, s_, N_BRANCH, D_MODEL)
    merged = (g[:, :, 0] * (o_a.reshape(b_, s_, A_W) @ w_oa)
              + g[:, :, 1] * (o_b.reshape(b_, s_, B_W) @ w_ob)
              + g[:, :, 2] * (o_c.reshape(b_, s_, C_W) @ w_oc))
    return merged @ w_out, new_rows


def sqrelu_mlp(h, w_up, w_down):
    return jnp.square(jax.nn.relu(h @ w_up)) @ w_down


def setup_inputs(seed: int = 0) -> dict:
    key = jax.random.key(seed)
    ks = iter(jax.random.split(key, 40))
    def nrm(shape, scale):
        return jax.random.normal(next(ks), shape, jnp.float32) * scale
    def gain(shape):
        return 1.0 + 0.02 * jax.random.normal(next(ks), shape, jnp.float32)
    c_win = min(C_REACH, PAST_LEN)
    return {
        'x_prompt': nrm((BATCH, SEQ, D_MODEL), 1.0),
        'x_sample': nrm((DEC_BATCH, DEC_SEQ, D_MODEL), 1.0),
        'cache_a_ckv': nrm((DEPTH, DEC_BATCH, PAST_LEN, A_KV_LORA), 1.0),
        'cache_a_krope': nrm((DEPTH, DEC_BATCH, PAST_LEN, A_ROPE), 1.0),
        'cache_b_k': nrm((DEPTH, DEC_BATCH, PAST_LEN, B_HEADS, B_HEAD_DIM), 1.0),
        'cache_b_v': nrm((DEPTH, DEC_BATCH, PAST_LEN, B_HEADS, B_HEAD_DIM), 1.0),
        'cache_b_idx_k': nrm((DEPTH, DEC_BATCH, PAST_LEN, IDX_DIM), 1.0),
        'cache_c_k': nrm((DEPTH, DEC_BATCH, c_win, C_HEADS, C_HEAD_DIM), 1.0),
        'cache_c_v': nrm((DEPTH, DEC_BATCH, c_win, C_HEADS, C_HEAD_DIM), 1.0),
        'norm1': gain((DEPTH, D_MODEL)),
        'w_in': nrm((DEPTH, D_MODEL, IN_DIM), D_MODEL ** -0.5),
        'a_q_norm': gain((DEPTH, A_Q_LORA)),
        'a_kv_norm': gain((DEPTH, A_KV_LORA)),
        'a_w_uq': nrm((DEPTH, A_Q_LORA, A_HEADS * (A_NOPE + A_ROPE)), A_Q_LORA ** -0.5),
        'a_w_uk': nrm((DEPTH, A_KV_LORA, A_HEADS * A_NOPE), A_KV_LORA ** -0.5),
        'a_w_uv': nrm((DEPTH, A_KV_LORA, A_HEADS * A_VDIM), A_KV_LORA ** -0.5),
        'c_rel_bias': nrm((DEPTH, C_HEADS, 2 * C_REL_CLIP + 1), 0.2),
        'w_oa': nrm((DEPTH, A_W, D_MODEL), A_W ** -0.5),
        'w_ob': nrm((DEPTH, B_W, D_MODEL), B_W ** -0.5),
        'w_oc': nrm((DEPTH, C_W, D_MODEL), C_W ** -0.5),
        'w_out': nrm((DEPTH, D_MODEL, D_MODEL), D_MODEL ** -0.5),
        'norm2': gain((DEPTH, D_MODEL)),
        'w_up': nrm((DEPTH, D_MODEL, D_FF), D_MODEL ** -0.5),
        'w_down': nrm((DEPTH, D_FF, D_MODEL), D_FF ** -0.5),
        'final_norm': gain((D_MODEL,)),
    }


def reference(x_prompt, x_sample, cache_a_ckv, cache_a_krope, cache_b_k, cache_b_v, cache_b_idx_k, cache_c_k, cache_c_v, norm1, w_in, a_q_norm, a_kv_norm, a_w_uq, a_w_uk, a_w_uv, c_rel_bias, w_oa, w_ob, w_oc, w_out, norm2, w_up, w_down, final_norm):
    past_len = cache_a_ckv.shape[2]
    pos_p = jnp.arange(x_prompt.shape[1], dtype=jnp.int32)
    pos_s = jnp.arange(past_len, past_len + x_sample.shape[1], dtype=jnp.int32)
    caches = (cache_a_ckv, cache_a_krope, cache_b_k, cache_b_v, cache_b_idx_k, cache_c_k, cache_c_v)
    xp, xs = x_prompt, x_sample
    rows_p, rows_s = [], []
    for l in range(DEPTH):
        weights = (w_in[l], a_q_norm[l], a_kv_norm[l], a_w_uq[l], a_w_uk[l], a_w_uv[l], c_rel_bias[l], w_oa[l], w_ob[l], w_oc[l], w_out[l])
        mix_p, new_p = token_mixers(rmsnorm(xp, norm1[l]), pos_p, None, *weights)
        mix_s, new_s = token_mixers(rmsnorm(xs, norm1[l]), pos_s, tuple(c[l] for c in caches), *weights)
        xp = xp + mix_p
        xs = xs + mix_s
        xp = xp + sqrelu_mlp(rmsnorm(xp, norm2[l]), w_up[l], w_down[l])
        xs = xs + sqrelu_mlp(rmsnorm(xs, norm2[l]), w_up[l], w_down[l])
        rows_p.append(new_p)
        rows_s.append(new_s)
    y_prompt = rmsnorm(xp, final_norm)
    y_sample = rmsnorm(xs, final_norm)
    stack = lambda rows, i: jnp.stack([r[i] for r in rows], axis=0)
    a_ckv_p, a_ckv_s = stack(rows_p, 0), stack(rows_s, 0)
    a_krope_p, a_krope_s = stack(rows_p, 1), stack(rows_s, 1)
    b_k_p, b_k_s = stack(rows_p, 2), stack(rows_s, 2)
    b_v_p, b_v_s = stack(rows_p, 3), stack(rows_s, 3)
    b_idx_k_p, b_idx_k_s = stack(rows_p, 4), stack(rows_s, 4)
    c_k_p, c_k_s = stack(rows_p, 5), stack(rows_s, 5)
    c_v_p, c_v_s = stack(rows_p, 6), stack(rows_s, 6)
    return (y_prompt, y_sample, a_ckv_p, a_ckv_s, a_krope_p, a_krope_s, b_k_p, b_k_s, b_v_p, b_v_s, b_idx_k_p, b_idx_k_s, c_k_p, c_k_s, c_v_p, c_v_s)
```

```python
import functools

import numpy as np
import jax
import jax.numpy as jnp
from jax import lax
from jax.experimental import pallas as pl
from jax.experimental.pallas import tpu as pltpu

F32 = jnp.float32
BF16 = jnp.bfloat16
I32 = jnp.int32

D_MODEL = 1024
CHUNK = 64
ROPE_THETA = 500000.0
RMS_EPS = 1e-6
HEADS = 8
A_Q_LORA = 256
A_KV_LORA = 128
A_NOPE = 64
A_ROPE = 32
A_VDIM = 64
A_SCALE = (A_NOPE + A_ROPE) ** -0.5
B_HEAD_DIM = 64
B_ROT = B_HEAD_DIM // 4
B_SCALE = B_HEAD_DIM ** -0.5
B_TOPK_MAX = 256
IDX_DIM = 64
IDX_ROT = IDX_DIM // 4
C_HEAD_DIM = 64
C_SCALE = C_HEAD_DIM ** -0.5
C_LEFT_CHUNKS = 8
C_REACH = C_LEFT_CHUNKS * CHUNK
C_REL_CLIP = 128
D_FF = 4 * D_MODEL
HW = HEADS * 64
LANES = 128
C_KEY_BLOCK = 128
C_BAND_BLOCKS = 5
C_BAND = C_KEY_BLOCK * C_BAND_BLOCKS
VMEM_LIMIT = 56 * 1024 * 1024
NEG_INF = float("-inf")
INT_MIN = -2 ** 31


def _rms(x, g):
    ms = jnp.mean(x * x, axis=-1, keepdims=True)
    return x * lax.rsqrt(ms + RMS_EPS) * g


def _rope_lanes(x, cos, sin, half):
    w = x.shape[-1]
    outs = []
    for j in range(w // LANES):
        xs = x[:, j * LANES:(j + 1) * LANES]
        lane = lax.broadcasted_iota(I32, xs.shape, 1)
        first = (lane % (2 * half)) < half
        partner = jnp.where(first, pltpu.roll(xs, LANES - half, 1), pltpu.roll(xs, half, 1))
        outs.append(xs * cos + partner * sin)
    return outs[0] if len(outs) == 1 else jnp.concatenate(outs, axis=1)


def _head_half_mask(shape, head):
    lane = lax.broadcasted_iota(I32, shape, 1)
    return (lane < 64) if head % 2 == 0 else (lane >= 64)


def _resident(arr):
    nd = arr.ndim
    return pl.BlockSpec(arr.shape, lambda *_: (0,) * nd, pipeline_mode=pl.Buffered(1))


def _params(n_axes):
    return pltpu.CompilerParams(dimension_semantics=("arbitrary",) * n_axes,
                                vmem_limit_bytes=VMEM_LIMIT)


def _prep_a_kernel(x_ref, g1_ref, w_ref, gq_ref, gkv_ref, wuq_ref, wukt_ref, cos_ref, sin_ref,
                   qa_ref, ka_ref, ckv_ref, kr_ref):
    xn = _rms(x_ref[...], g1_ref[...]).astype(BF16)
    z = jnp.dot(xn, w_ref[...], preferred_element_type=F32)
    aq = z[:, :A_Q_LORA]
    akv = z[:, A_Q_LORA:A_Q_LORA + A_KV_LORA]
    akr = z[:, A_Q_LORA + A_KV_LORA:]
    cos = cos_ref[...]
    sin = sin_ref[...]
    ckv = _rms(akv, gkv_ref[...])
    ckv_ref[...] = ckv
    kr = _rope_lanes(akr, cos, sin, A_ROPE // 2)
    kr_ref[...] = kr[:, :A_ROPE]
    ka_ref[...] = jnp.concatenate([ckv, kr], axis=1).astype(BF16)
    aqn = _rms(aq, gq_ref[...]).astype(BF16)
    qa = jnp.dot(aqn, wuq_ref[...], preferred_element_type=F32)
    for h in range(HEADS):
        pair = qa[:, (h // 2) * LANES:(h // 2 + 1) * LANES].astype(BF16)
        q_lat = jnp.dot(pair, wukt_ref[h], preferred_element_type=F32)
        q_rope = _rope_lanes(qa[:, HW + h * LANES:HW + (h + 1) * LANES], cos, sin, A_ROPE // 2)
        qa_ref[:, h * 256:(h + 1) * 256] = (
            jnp.concatenate([q_lat, q_rope], axis=1) * A_SCALE).astype(BF16)


def _prep_b_kernel(x_ref, g1_ref, w_ref, cos_ref, sin_ref, cosk_ref, sink_ref,
                   bq_ref, bk_ref, bv_ref, iq_ref, ik2_ref, iw_ref, bk32_ref, bv32_ref, ik32_ref):
    xn = _rms(x_ref[...], g1_ref[...]).astype(BF16)
    z = jnp.dot(xn, w_ref[...], preferred_element_type=F32)
    cos = cos_ref[...]
    sin = sin_ref[...]
    bq = _rope_lanes(z[:, 0:HW], cos, sin, B_ROT // 2)
    bq_ref[...] = (bq * B_SCALE).astype(BF16)
    bk = _rope_lanes(z[:, HW:2 * HW], cos, sin, B_ROT // 2)
    bk32_ref[...] = bk
    bk_ref[...] = bk.astype(BF16)
    bv = z[:, 2 * HW:3 * HW]
    bv32_ref[...] = bv
    bv_ref[...] = bv.astype(BF16)
    iq = _rope_lanes(z[:, 3 * HW:4 * HW], cos, sin, IDX_ROT // 2)
    iq_ref[...] = (iq * IDX_DIM ** -0.5).astype(BF16)
    ik = _rope_lanes(z[:, 4 * HW:4 * HW + LANES], cosk_ref[...], sink_ref[...], IDX_ROT // 2)
    ik32_ref[...] = ik[:, :IDX_DIM]
    lane = lax.broadcasted_iota(I32, ik.shape, 1)
    ik2_ref[...] = jnp.where(lane < IDX_DIM, ik, pltpu.roll(ik, IDX_DIM, 1)).astype(BF16)
    iw_ref[...] = z[:, 4 * HW + LANES:] * HEADS ** -0.5


def _prep_c_kernel(x_ref, g1_ref, w_ref, cq_ref, ck_ref, cv_ref):
    xn = _rms(x_ref[...], g1_ref[...]).astype(BF16)
    z = jnp.dot(xn, w_ref[...], preferred_element_type=F32)
    cq_ref[...] = (z[:, 0:HW] * C_SCALE).astype(BF16)
    ck_ref[...] = z[:, HW:2 * HW].astype(BF16)
    cv_ref[...] = z[:, 2 * HW:].astype(BF16)


def _prep_c32_kernel(x_ref, g1_ref, w_ref, ck_ref, cv_ref):
    xn = _rms(x_ref[...], g1_ref[...]).astype(BF16)
    z = jnp.dot(xn, w_ref[...], preferred_element_type=F32)
    ck_ref[...] = z[:, 0:HW]
    cv_ref[...] = z[:, HW:]


def _row_call(kernel, x, tm, operands, specs, outs, name):
    t = x.shape[0]
    name = f"{name}_t{t}"
    grid = (t // tm,)
    in_specs = [pl.BlockSpec((tm, x.shape[1]), lambda i: (i, 0))] + specs
    out_shape = [jax.ShapeDtypeStruct((t, w), dt) for w, dt in outs]
    out_specs = [pl.BlockSpec((tm, w), lambda i: (i, 0)) for w, _ in outs]
    return pl.pallas_call(kernel, grid=grid, in_specs=in_specs, out_specs=out_specs,
                          out_shape=out_shape, compiler_params=_params(1), name=name)(x, *operands)


def _table_spec(table, tm):
    nblk = table.shape[0] // tm
    return pl.BlockSpec((tm, table.shape[1]), lambda i: (i % nblk, 0))


def _attn_a_kernel(q_ref, k_ref, wuv_ref, o_ref, m_ref, l_ref, acc_ref, *, tq, tk, q_pos0, l_valid):
    qi = pl.program_id(1)
    qs = jnp.concatenate([q_ref[:, h * 256:(h + 1) * 256] for h in range(HEADS)], axis=0)
    qpos = q_pos0 + qi * tq
    limit = jnp.minimum((qpos // CHUNK + 1) * CHUNK, l_valid)
    nkb = (limit + tk - 1) // tk
    m_ref[...] = jnp.full(m_ref.shape, NEG_INF, F32)
    l_ref[...] = jnp.zeros(l_ref.shape, F32)
    acc_ref[...] = jnp.zeros(acc_ref.shape, F32)

    def body(kb, carry):
        start = pl.multiple_of(kb * tk, tk)
        kblk = k_ref[pl.ds(start, tk), :]
        s = lax.dot_general(qs, kblk, (((1,), (1,)), ((), ())), preferred_element_type=F32)
        kpos = start + lax.broadcasted_iota(I32, (1, tk), 1)
        s = jnp.where(kpos < limit, s, NEG_INF)
        m_old = m_ref[...]
        m_new = jnp.maximum(m_old, jnp.max(s, axis=1, keepdims=True))
        alpha = jnp.exp(m_old - m_new)
        p = jnp.exp(s - m_new)
        l_ref[...] = alpha * l_ref[...] + jnp.sum(p, axis=1, keepdims=True)
        acc_ref[...] = alpha * acc_ref[...] + jnp.dot(
            p.astype(BF16), kblk[:, :A_KV_LORA], preferred_element_type=F32)
        m_ref[...] = m_new
        return carry

    lax.fori_loop(0, nkb, body, 0)
    o_lat = (acc_ref[...] / l_ref[...]).astype(BF16)
    o = jnp.zeros((tq, HW), F32)
    for h in range(HEADS):
        o = o + jnp.dot(o_lat[h * tq:(h + 1) * tq], wuv_ref[h], preferred_element_type=F32)
    o_ref[...] = o.astype(BF16)


def _attn_a(qa, ka, wuv_pad, *, nb, s_q, l_keys, tq, tk, q_pos0, l_valid):
    nq = s_q // tq
    kern = functools.partial(_attn_a_kernel, tq=tq, tk=tk, q_pos0=q_pos0, l_valid=l_valid)
    return pl.pallas_call(
        kern, grid=(nb, nq),
        in_specs=[pl.BlockSpec((tq, HEADS * 256), lambda b, i: (b * nq + i, 0)),
                  pl.BlockSpec((l_keys, 256), lambda b, i: (b, 0), pipeline_mode=pl.Buffered(1)),
                  _resident(wuv_pad)],
        out_specs=pl.BlockSpec((tq, HW), lambda b, i: (b * nq + i, 0)),
        out_shape=jax.ShapeDtypeStruct((nb * s_q, HW), BF16),
        scratch_shapes=[pltpu.VMEM((HEADS * tq, 1), F32), pltpu.VMEM((HEADS * tq, 1), F32),
                        pltpu.VMEM((HEADS * tq, A_KV_LORA), F32)],
        compiler_params=_params(2), name=f"attn_a_q{s_q}")(qa, ka, wuv_pad)


def _sortable_key(x):
    bits = lax.bitcast_convert_type(x, I32)
    return bits ^ ((bits >> 31) & 0x7FFFFFFF)


def _attn_b_kernel(q_ref, iq_ref, iw_ref, k_ref, v_ref, ik_ref, o_ref, s_ref, m_ref, l_ref, acc_ref,
                   *, tq, tk, q_pos0, l_valid, topk, idx_bits):
    qi = pl.program_id(1)
    qpos0 = q_pos0 + qi * tq
    row = lax.broadcasted_iota(I32, (tq, 1), 0)
    limit_row = jnp.minimum(((qpos0 + row) // CHUNK + 1) * CHUNK, l_valid)
    limit_max = jnp.minimum(((qpos0 + tq - 1) // CHUNK + 1) * CHUNK, l_valid)
    nkb = (limit_max + tk - 1) // tk
    kf = float(topk)

    iq = iq_ref[...]
    iqs = jnp.concatenate(
        [jnp.where(_head_half_mask((tq, LANES), h), iq[:, (h // 2) * LANES:(h // 2 + 1) * LANES],
                   jnp.zeros((tq, LANES), BF16)) for h in range(HEADS)], axis=0)
    w = iw_ref[...]

    def score_body(kb, carry):
        start = pl.multiple_of(kb * tk, tk)
        x = lax.dot_general(iqs, ik_ref[pl.ds(start, tk), :], (((1,), (1,)), ((), ())),
                            preferred_element_type=F32)
        sc = jnp.zeros((tq, tk), F32)
        for h in range(HEADS):
            sc = sc + jnp.maximum(x[h * tq:(h + 1) * tq], 0.0) * w[:, h:h + 1]
        kpos = start + lax.broadcasted_iota(I32, (1, tk), 1)
        sc = jnp.where(kpos < limit_row, sc + 0.0, NEG_INF)
        s_ref[kb] = _sortable_key(sc)
        return carry

    lax.fori_loop(0, nkb, score_body, 0)

    def count(pred):
        def body(kb, c):
            blk = s_ref[kb]
            kpos = kb * tk + lax.broadcasted_iota(I32, (1, tk), 1)
            ind = jnp.where(pred(blk, kpos), 1.0, 0.0)
            for j in range(tk // LANES):
                c = c + ind[:, j * LANES:(j + 1) * LANES]
            return c
        c = lax.fori_loop(0, nkb, body, jnp.zeros((tq, LANES), F32))
        return jnp.sum(c, axis=1, keepdims=True)

    thr = jnp.where(count(lambda blk, kpos: blk >= 0) >= kf, 0, INT_MIN).astype(I32)

    def bit_body(i, t):
        cand = t + (jnp.int32(1) << (30 - i))
        return jnp.where(count(lambda blk, kpos: blk >= cand) >= kf, cand, t)

    thr = lax.fori_loop(0, 31, bit_body, thr)
    n_gt = count(lambda blk, kpos: blk > thr)
    n_ge = count(lambda blk, kpos: blk >= thr)
    need = kf - n_gt
    excess = jnp.max(jnp.where((n_ge - n_gt) > need, 1.0, 0.0)) > 0.0

    def tie_cut(_):
        def jb(i, j):
            cand = j + (jnp.int32(1) << (idx_bits - 1 - i))
            c = count(lambda blk, kpos: (blk == thr) & (kpos < cand))
            return jnp.where(c < need, cand, j)
        return lax.fori_loop(0, idx_bits, jb, jnp.zeros((tq, 1), I32))

    cut = lax.cond(excess, tie_cut, lambda _: jnp.full((tq, 1), 2 ** 30, I32), 0)

    def bias_body(kb, carry):
        blk = s_ref[kb]
        kpos = kb * tk + lax.broadcasted_iota(I32, (1, tk), 1)
        sel = ((blk > thr) | ((blk == thr) & (kpos <= cut))) & (kpos < limit_row)
        s_ref[kb] = lax.bitcast_convert_type(jnp.where(sel, 0.0, NEG_INF).astype(F32), I32)
        return carry

    lax.fori_loop(0, nkb, bias_body, 0)

    q = q_ref[...]
    o_even = None
    for h in range(HEADS):
        pr = h // 2
        qm = jnp.where(_head_half_mask((tq, LANES), h), q[:, pr * LANES:(pr + 1) * LANES],
                       jnp.zeros((tq, LANES), BF16))
        m_ref[...] = jnp.full(m_ref.shape, NEG_INF, F32)
        l_ref[...] = jnp.zeros(l_ref.shape, F32)
        acc_ref[...] = jnp.zeros(acc_ref.shape, F32)

        def body(kb, carry, qm=qm, pr=pr):
            start = pl.multiple_of(kb * tk, tk)
            kblk = k_ref[pl.ds(start, tk), pr * LANES:(pr + 1) * LANES]
            vblk = v_ref[pl.ds(start, tk), pr * LANES:(pr + 1) * LANES]
            s = lax.dot_general(qm, kblk, (((1,), (1,)), ((), ())), preferred_element_type=F32)
            s = s + lax.bitcast_convert_type(s_ref[kb], F32)
            m_old = m_ref[...]
            m_new = jnp.maximum(m_old, jnp.max(s, axis=1, keepdims=True))
            m_safe = jnp.where(m_new == NEG_INF, 0.0, m_new)
            alpha = jnp.exp(m_old - m_safe)
            p = jnp.exp(s - m_safe)
            l_ref[...] = alpha * l_ref[...] + jnp.sum(p, axis=1, keepdims=True)
            acc_ref[...] = alpha * acc_ref[...] + jnp.dot(p.astype(BF16), vblk, preferred_element_type=F32)
            m_ref[...] = m_new
            return carry

        lax.fori_loop(0, nkb, body, 0)
        o_h = acc_ref[...] / l_ref[...]
        if h % 2 == 0:
            o_even = o_h
        else:
            lane = lax.broadcasted_iota(I32, (tq, LANES), 1)
            o_ref[:, pr * LANES:(pr + 1) * LANES] = jnp.where(lane < 64, o_even, o_h).astype(BF16)


def _attn_b(bq, iq, iw, bk, bv, ik2, *, nb, s_q, l_keys, tq, tk, q_pos0, l_valid, topk):
    nq = s_q // tq
    nkb_max = l_keys // tk
    idx_bits = max(1, int(l_keys - 1).bit_length())
    kern = functools.partial(_attn_b_kernel, tq=tq, tk=tk, q_pos0=q_pos0, l_valid=l_valid,
                             topk=topk, idx_bits=idx_bits)
    qmap = lambda b, i: (b * nq + i, 0)
    kmap = lambda b, i: (b, 0)
    return pl.pallas_call(
        kern, grid=(nb, nq),
        in_specs=[pl.BlockSpec((tq, HW), qmap), pl.BlockSpec((tq, HW), qmap),
                  pl.BlockSpec((tq, LANES), qmap),
                  pl.BlockSpec((l_keys, HW), kmap, pipeline_mode=pl.Buffered(1)),
                  pl.BlockSpec((l_keys, HW), kmap, pipeline_mode=pl.Buffered(1)),
                  pl.BlockSpec((l_keys, LANES), kmap, pipeline_mode=pl.Buffered(1))],
        out_specs=pl.BlockSpec((tq, HW), qmap),
        out_shape=jax.ShapeDtypeStruct((nb * s_q, HW), BF16),
        scratch_shapes=[pltpu.VMEM((nkb_max, tq, tk), I32),
                        pltpu.VMEM((tq, 1), F32), pltpu.VMEM((tq, 1), F32),
                        pltpu.VMEM((tq, LANES), F32)],
        compiler_params=_params(2), name=f"attn_b_q{s_q}")(bq, iq, iw, bk, bv, ik2)


def _attn_c_kernel(*refs, tq, off):
    q_ref = refs[0]
    k_refs = refs[1:1 + C_BAND_BLOCKS]
    v_refs = refs[1 + C_BAND_BLOCKS:1 + 2 * C_BAND_BLOCKS]
    bias_ref = refs[1 + 2 * C_BAND_BLOCKS]
    o_ref = refs[2 + 2 * C_BAND_BLOCKS]
    kc_ref, vc_ref = refs[3 + 2 * C_BAND_BLOCKS:]
    qi = pl.program_id(1)
    for j in range(C_BAND_BLOCKS):
        kc_ref[j * C_KEY_BLOCK:(j + 1) * C_KEY_BLOCK, :] = k_refs[j][...]
        vc_ref[j * C_KEY_BLOCK:(j + 1) * C_KEY_BLOCK, :] = v_refs[j][...]
    col = lax.broadcasted_iota(I32, (1, C_BAND), 1)
    col_valid = col >= (off - qi) * C_KEY_BLOCK
    q = q_ref[...]
    o_even = None
    for h in range(HEADS):
        pr = h // 2
        qm = jnp.where(_head_half_mask((tq, LANES), h), q[:, pr * LANES:(pr + 1) * LANES],
                       jnp.zeros((tq, LANES), BF16))
        s = lax.dot_general(qm, kc_ref[:, pr * LANES:(pr + 1) * LANES], (((1,), (1,)), ((), ())),
                            preferred_element_type=F32) + bias_ref[h]
        s = jnp.where(col_valid, s, NEG_INF)
        m = jnp.max(s, axis=1, keepdims=True)
        p = jnp.exp(s - m)
        l = jnp.sum(p, axis=1, keepdims=True)
        o_h = jnp.dot(p.astype(BF16), vc_ref[:, pr * LANES:(pr + 1) * LANES],
                      preferred_element_type=F32) / l
        if h % 2 == 0:
            o_even = o_h
        else:
            lane = lax.broadcasted_iota(I32, (tq, LANES), 1)
            o_ref[:, pr * LANES:(pr + 1) * LANES] = jnp.where(lane < 64, o_even, o_h).astype(BF16)


def _attn_c(cq, ck, cv, bias, *, nb, s_q, l_keys, tq, off):
    nq = s_q // tq
    nkblk = l_keys // C_KEY_BLOCK
    kern = functools.partial(_attn_c_kernel, tq=tq, off=off)
    qmap = lambda b, i: (b * nq + i, 0)

    def kspec(j):
        return pl.BlockSpec((C_KEY_BLOCK, HW), lambda b, i: (b * nkblk + jnp.maximum(i + j - off, 0), 0))

    kv_specs = [kspec(j) for j in range(C_BAND_BLOCKS)]
    return pl.pallas_call(
        kern, grid=(nb, nq),
        in_specs=[pl.BlockSpec((tq, HW), qmap)] + kv_specs + kv_specs + [_resident(bias)],
        out_specs=pl.BlockSpec((tq, HW), qmap),
        out_shape=jax.ShapeDtypeStruct((nb * s_q, HW), BF16),
        scratch_shapes=[pltpu.VMEM((C_BAND, HW), BF16), pltpu.VMEM((C_BAND, HW), BF16)],
        compiler_params=_params(2), name=f"attn_c_q{s_q}")(
            cq, *([ck] * C_BAND_BLOCKS), *([cv] * C_BAND_BLOCKS), bias)


def _merge_kernel(x_ref, g1_ref, oa_ref, ob_ref, oc_ref, wg_ref, woa_ref, wob_ref, woc_ref, wout_ref, y_ref):
    x = x_ref[...]
    xn = _rms(x, g1_ref[...]).astype(BF16)
    merged = jnp.zeros(x.shape, F32)
    for b, (o_ref, w_ref) in enumerate(((oa_ref, woa_ref), (ob_ref, wob_ref), (oc_ref, woc_ref))):
        gate = jax.nn.sigmoid(jnp.dot(xn, wg_ref[:, b * D_MODEL:(b + 1) * D_MODEL],
                                      preferred_element_type=F32))
        merged = merged + gate * jnp.dot(o_ref[...], w_ref[...], preferred_element_type=F32)
    y_ref[...] = x + jnp.dot(merged.astype(BF16), wout_ref[...], preferred_element_type=F32)


def _mlp_kernel(x_ref, g2_ref, wup_ref, wdn_ref, gf_ref, y_ref, *, ff_chunk, final):
    x = x_ref[...]
    xn = _rms(x, g2_ref[...]).astype(BF16)
    acc = jnp.zeros(x.shape, F32)
    for c in range(D_FF // ff_chunk):
        h = jnp.dot(xn, wup_ref[:, c * ff_chunk:(c + 1) * ff_chunk], preferred_element_type=F32)
        h = jnp.square(jnp.maximum(h, 0.0)).astype(BF16)
        acc = acc + jnp.dot(h, wdn_ref[c * ff_chunk:(c + 1) * ff_chunk, :], preferred_element_type=F32)
    y = x + acc
    if final:
        y = _rms(y, gf_ref[...])
    y_ref[...] = y


def _rope_tables(pos, rot, period, live_lanes):
    half = rot // 2
    inv_freq = ROPE_THETA ** (-jnp.arange(half, dtype=F32) / half)
    ang = pos.astype(F32)[:, None] * inv_freq[None, :]
    cos, sin = jnp.cos(ang), jnp.sin(ang)
    lane = np.arange(LANES)
    r = lane % period
    live = (lane < live_lanes) & (r < rot)
    fi = np.where(live, r % half, 0)
    sign = np.where(r < half, -1.0, 1.0).astype(np.float32)
    cos_t = jnp.where(live[None, :], cos[:, fi], 1.0)
    sin_t = jnp.where(live[None, :], sin[:, fi] * sign[None, :], 0.0)
    return cos_t, sin_t


def _layer_weights(w_in, a_w_uq, a_w_uk, a_w_uv):
    offs = np.cumsum([0, A_Q_LORA, A_KV_LORA, A_ROPE, HW, HW, HW, HW, IDX_DIM, HEADS, HW, HW, HW,
                      3 * D_MODEL])
    seg = lambda i: w_in[:, offs[i]:offs[i + 1]]
    zeros = lambda n: jnp.zeros((D_MODEL, n), w_in.dtype)
    w_a = jnp.concatenate([seg(0), seg(1), seg(2), zeros(LANES - A_ROPE)], axis=1)
    w_b = jnp.concatenate([seg(3), seg(4), seg(5), seg(6), seg(7), zeros(LANES - IDX_DIM),
                           seg(8), zeros(LANES - HEADS)], axis=1)
    w_c = jnp.concatenate([seg(9), seg(10), seg(11)], axis=1)
    w_g = seg(12)
    uq = a_w_uq.reshape(A_Q_LORA, HEADS, A_NOPE + A_ROPE)
    uq_nope = uq[:, :, :A_NOPE].reshape(A_Q_LORA, HW)
    uq_rope = jnp.pad(uq[:, :, A_NOPE:], ((0, 0), (0, 0), (0, LANES - A_ROPE))).reshape(A_Q_LORA, HEADS * LANES)
    w_uq = jnp.concatenate([uq_nope, uq_rope], axis=1)
    uk = a_w_uk.reshape(A_KV_LORA, HEADS, A_NOPE)
    ukt = jnp.transpose(uk, (1, 2, 0))
    ukt_pad = jnp.stack([jnp.pad(ukt[h], ((0, 64), (0, 0)) if h % 2 == 0 else ((64, 0), (0, 0)))
                         for h in range(HEADS)], axis=0)
    uv = a_w_uv.reshape(A_KV_LORA, HEADS, A_VDIM)
    uv_pad = jnp.stack([jnp.pad(uv[:, h], ((0, 0), (h * A_VDIM, HW - (h + 1) * A_VDIM)))
                        for h in range(HEADS)], axis=0)
    bf = lambda a: a.astype(BF16)
    return bf(w_a), bf(w_b), bf(w_c), bf(w_g), bf(w_uq), bf(ukt_pad), bf(uv_pad)


def _band_bias(rel_bias, q_pos, k_pos, k_live):
    q_c, k_c = q_pos // CHUNK, k_pos // CHUNK
    vis = (k_c[None, :] <= q_c[:, None]) & (k_c[None, :] >= q_c[:, None] - C_LEFT_CHUNKS) & k_live[None, :]
    idx = np.clip(q_pos[:, None] - k_pos[None, :], -C_REL_CLIP, C_REL_CLIP) + C_REL_CLIP
    bias = rel_bias[:, idx].astype(F32)
    return jnp.where(vis[None], bias, NEG_INF)


def _pad_rows(a, n):
    return jnp.pad(a, ((0, 0), (0, n - a.shape[1]), (0, 0)))


def _mixers(x2d, lw, *, nb, s_q, q_pos0, past, tm, tq_a, tq_b, tq_c, tk_a, tk_b, tables, c_rel_bias):
    (g1, gq, gkv, w_a, w_b, w_c, w_uq, ukt_pad, uv_pad) = lw
    cos_a, sin_a, cos_b, sin_b, cos_k, sin_k = tables
    t = nb * s_q
    res = _resident
    qa, ka, ckv32, kr32 = _row_call(
        _prep_a_kernel, x2d, tm, (g1, w_a, gq, gkv, w_uq, ukt_pad, cos_a, sin_a),
        [res(g1), res(w_a), res(gq), res(gkv), res(w_uq), res(ukt_pad), _table_spec(cos_a, tm),
         _table_spec(sin_a, tm)],
        [(HEADS * 256, BF16), (256, BF16), (A_KV_LORA, F32), (A_ROPE, F32)], "prep_a")
    bq, bk, bv, iq, ik2, iw, bk32, bv32, ik32 = _row_call(
        _prep_b_kernel, x2d, tm, (g1, w_b, cos_b, sin_b, cos_k, sin_k),
        [res(g1), res(w_b), _table_spec(cos_b, tm), _table_spec(sin_b, tm), _table_spec(cos_k, tm),
         _table_spec(sin_k, tm)],
        [(HW, BF16), (HW, BF16), (HW, BF16), (HW, BF16), (LANES, BF16), (LANES, F32),
         (HW, F32), (HW, F32), (IDX_DIM, F32)], "prep_b")
    cq, ck, cv = _row_call(_prep_c_kernel, x2d, tm, (g1, w_c), [res(g1), res(w_c)],
                           [(HW, BF16), (HW, BF16), (HW, BF16)], "prep_c")

    if past is None:
        l_valid = s_q
        l_keys = s_q
        keys_a, keys_bk, keys_bv, keys_ik = ka, bk, bv, ik2
        keys_ck, keys_cv = ck, cv
        lc_keys = s_q
        c_off = C_BAND_BLOCKS - 1
        qp = np.arange(tq_c)
        kp = np.arange(C_BAND) - C_REACH
        bias = _band_bias(c_rel_bias, qp, kp, np.ones(C_BAND, bool))
        keep = min(C_REACH, s_q)
        x_tail = x2d.reshape(nb, s_q, D_MODEL)[:, s_q - keep:].reshape(nb * keep, D_MODEL)
        w_ckv = w_c[:, HW:]
        ck32, cv32 = _row_call(_prep_c32_kernel, x_tail, min(tm, nb * keep), (g1, w_ckv),
                               [res(g1), res(w_ckv)], [(HW, F32), (HW, F32)], "prep_c32")
        ck32 = ck32.reshape(nb, keep, HEADS, C_HEAD_DIM)
        cv32 = cv32.reshape(nb, keep, HEADS, C_HEAD_DIM)
    else:
        p_ckv, p_kr, p_bk, p_bv, p_ik, p_ck, p_cv = past
        past_len = p_ckv.shape[1]
        l_valid = past_len + s_q
        l_keys = -(-l_valid // LANES) * LANES
        r3 = lambda a: a.reshape(nb, s_q, a.shape[-1])
        cat = lambda old, new: _pad_rows(jnp.concatenate([old.astype(BF16), r3(new)], axis=1), l_keys)
        flat = lambda a: a.reshape(nb * a.shape[1], a.shape[2])
        old_a = jnp.concatenate([p_ckv, p_kr, jnp.zeros((nb, past_len, 256 - A_KV_LORA - A_ROPE), F32)], axis=-1)
        keys_a = flat(cat(old_a, ka))
        keys_bk = flat(cat(p_bk.reshape(nb, past_len, HW), bk))
        keys_bv = flat(cat(p_bv.reshape(nb, past_len, HW), bv))
        keys_ik = flat(cat(jnp.concatenate([p_ik, p_ik], axis=-1), ik2))
        w_c_len = p_ck.shape[1]
        lc_valid = w_c_len + s_q
        lc_keys = C_BAND
        catc = lambda old, new: _pad_rows(jnp.concatenate([old.astype(BF16), r3(new)], axis=1), lc_keys)
        keys_ck = flat(catc(p_ck.reshape(nb, w_c_len, HW), ck))
        keys_cv = flat(catc(p_cv.reshape(nb, w_c_len, HW), cv))
        c_off = 0
        qp = np.arange(q_pos0, q_pos0 + s_q)
        kp = np.arange(past_len - w_c_len, past_len - w_c_len + lc_keys)
        bias = _band_bias(c_rel_bias, qp, kp, np.arange(lc_keys) < lc_valid)
        ck32, cv32 = _row_call(_prep_c32_kernel, x2d, tm, (g1, w_c[:, HW:]),
                               [res(g1), res(w_c[:, HW:])], [(HW, F32), (HW, F32)], "prep_c32")
        ck32 = ck32.reshape(nb, s_q, HEADS, C_HEAD_DIM)
        cv32 = cv32.reshape(nb, s_q, HEADS, C_HEAD_DIM)

    topk = min(B_TOPK_MAX, l_valid // 4)
    tk_a = min(tk_a, l_keys)
    tk_b = min(tk_b, l_keys)
    o_a = _attn_a(qa, keys_a, uv_pad, nb=nb, s_q=s_q, l_keys=l_keys, tq=tq_a, tk=tk_a,
                  q_pos0=q_pos0, l_valid=l_valid)
    o_b = _attn_b(bq, iq, iw, keys_bk, keys_bv, keys_ik, nb=nb, s_q=s_q, l_keys=l_keys, tq=tq_b,
                  tk=tk_b, q_pos0=q_pos0, l_valid=l_valid, topk=topk)
    o_c = _attn_c(cq, keys_ck, keys_cv, bias, nb=nb, s_q=s_q, l_keys=lc_keys, tq=tq_c, off=c_off)
    new_rows = (ckv32.reshape(nb, s_q, A_KV_LORA), kr32.reshape(nb, s_q, A_ROPE),
                bk32.reshape(nb, s_q, HEADS, B_HEAD_DIM), bv32.reshape(nb, s_q, HEADS, B_HEAD_DIM),
                ik32.reshape(nb, s_q, IDX_DIM), ck32, cv32)
    return o_a, o_b, o_c, new_rows


def _layer(x2d, lw_mix, lw_rest, gf, *, final, tm, **kw):
    o_a, o_b, o_c, new_rows = _mixers(x2d, lw_mix, tm=tm, **kw)
    g1 = lw_mix[0]
    (w_g, w_oa, w_ob, w_oc, w_out, g2, w_up, w_dn) = lw_rest
    res = _resident
    row = lambda w: pl.BlockSpec((tm, w), lambda i: (i, 0))
    (x1,) = _row_call(_merge_kernel, x2d, tm, (g1, o_a, o_b, o_c, w_g, w_oa, w_ob, w_oc, w_out),
                      [res(g1), row(HW), row(HW), row(HW), res(w_g), res(w_oa), res(w_ob), res(w_oc),
                       res(w_out)], [(D_MODEL, F32)], "merge")
    (x2,) = _row_call(functools.partial(_mlp_kernel, ff_chunk=1024, final=final), x1, tm,
                      (g2, w_up, w_dn, gf), [res(g2), res(w_up), res(w_dn), res(gf)], [(D_MODEL, F32)],
                      "mlp_final" if final else "mlp")
    return x2, new_rows


def kernel(x_prompt, x_sample, cache_a_ckv, cache_a_krope, cache_b_k, cache_b_v, cache_b_idx_k, cache_c_k, cache_c_v, norm1, w_in, a_q_norm, a_kv_norm, a_w_uq, a_w_uk, a_w_uv, c_rel_bias, w_oa, w_ob, w_oc, w_out, norm2, w_up, w_down, final_norm):
    depth = w_in.shape[0]
    nb_p, s_p, _ = x_prompt.shape
    nb_s, s_s, _ = x_sample.shape
    past_len = cache_a_ckv.shape[2]
    assert s_p % 256 == 0 and CHUNK % s_s == 0 and past_len % CHUNK == 0
    tm_p = 512 if (nb_p * s_p) % 512 == 0 else 256
    tm_s = nb_s * s_s

    pos_p = jnp.arange(s_p, dtype=jnp.int32)
    pos_s = jnp.tile(jnp.arange(past_len, past_len + s_s, dtype=jnp.int32), nb_s)

    def tables(pos):
        return (_rope_tables(pos, A_ROPE, A_ROPE, A_ROPE) + _rope_tables(pos, B_ROT, B_HEAD_DIM, LANES)
                + _rope_tables(pos, IDX_ROT, IDX_DIM, IDX_DIM))

    tab_p, tab_s = tables(pos_p), tables(pos_s)
    gf = final_norm.reshape(1, D_MODEL)
    bf = lambda a: a.astype(BF16)

    xp = x_prompt.reshape(nb_p * s_p, D_MODEL)
    xs = x_sample.reshape(nb_s * s_s, D_MODEL)
    rows_p, rows_s = [], []
    for l in range(depth):
        w_a, w_b, w_c, w_g, w_uq, ukt_pad, uv_pad = _layer_weights(w_in[l], a_w_uq[l], a_w_uk[l], a_w_uv[l])
        lw_mix = (norm1[l].reshape(1, -1), a_q_norm[l].reshape(1, -1), a_kv_norm[l].reshape(1, -1),
                  w_a, w_b, w_c, w_uq, ukt_pad, uv_pad)
        lw_rest = (w_g, bf(w_oa[l]), bf(w_ob[l]), bf(w_oc[l]), bf(w_out[l]), norm2[l].reshape(1, -1),
                   bf(w_up[l]), bf(w_down[l]))
        final = l == depth - 1
        past = (cache_a_ckv[l], cache_a_krope[l], cache_b_k[l], cache_b_v[l], cache_b_idx_k[l],
                cache_c_k[l], cache_c_v[l])
        xp, new_p = _layer(xp, lw_mix, lw_rest, gf, final=final, tm=tm_p, nb=nb_p, s_q=s_p, q_pos0=0,
                           past=None, tq_a=64, tq_b=256, tq_c=128, tk_a=1024, tk_b=512,
                           tables=tab_p, c_rel_bias=c_rel_bias[l])
        xs, new_s = _layer(xs, lw_mix, lw_rest, gf, final=final, tm=tm_s, nb=nb_s, s_q=s_s,
                           q_pos0=past_len, past=past, tq_a=s_s, tq_b=s_s, tq_c=s_s, tk_a=1 << 30,
                           tk_b=1 << 30, tables=tab_s, c_rel_bias=c_rel_bias[l])
        rows_p.append(new_p)
        rows_s.append(new_s)

    y_prompt = xp.reshape(nb_p, s_p, D_MODEL)
    y_sample = xs.reshape(nb_s, s_s, D_MODEL)
    stack = lambda rows, i: jnp.stack([r[i] for r in rows], axis=0)
    outs = [y_prompt, y_sample]
    for i in range(7):
        outs += [stack(rows_p, i), stack(rows_s, i)]
    return tuple(outs)
```

```python
import functools

import numpy as np
import jax
import jax.numpy as jnp
from jax import lax
from jax.experimental import pallas as pl
from jax.experimental.pallas import tpu as pltpu

F32 = jnp.float32
BF16 = jnp.bfloat16
I32 = jnp.int32

D_MODEL = 1024
CHUNK = 64
ROPE_THETA = 500000.0
RMS_EPS = 1e-6
HEADS = 8
A_Q_LORA = 256
A_KV_LORA = 128
A_NOPE = 64
A_ROPE = 32
A_VDIM = 64
A_SCALE = (A_NOPE + A_ROPE) ** -0.5
B_HEAD_DIM = 64
B_ROT = B_HEAD_DIM // 4
B_SCALE = B_HEAD_DIM ** -0.5
B_TOPK_MAX = 256
IDX_DIM = 64
IDX_ROT = IDX_DIM // 4
C_HEAD_DIM = 64
C_SCALE = C_HEAD_DIM ** -0.5
C_LEFT_CHUNKS = 8
C_REACH = C_LEFT_CHUNKS * CHUNK
C_REL_CLIP = 128
D_FF = 4 * D_MODEL
HW = HEADS * 64
LANES = 128
C_KEY_BLOCK = 128
C_BAND_BLOCKS = 5
C_BAND = C_KEY_BLOCK * C_BAND_BLOCKS
VMEM_LIMIT = 56 * 1024 * 1024
NEG_INF = float("-inf")
LOG2E = 1.4426950408889634
INT_MIN = -2 ** 31


def _rms(x, g):
    ms = jnp.mean(x * x, axis=-1, keepdims=True)
    return x * lax.rsqrt(ms + RMS_EPS) * g


def _rope_lanes(x, cos, sin, half):
    w = x.shape[-1]
    outs = []
    for j in range(w // LANES):
        xs = x[:, j * LANES:(j + 1) * LANES]
        lane = lax.broadcasted_iota(I32, xs.shape, 1)
        first = (lane % (2 * half)) < half
        partner = jnp.where(first, pltpu.roll(xs, LANES - half, 1), pltpu.roll(xs, half, 1))
        outs.append(xs * cos + partner * sin)
    return outs[0] if len(outs) == 1 else jnp.concatenate(outs, axis=1)


def _head_half_mask(shape, head):
    lane = lax.broadcasted_iota(I32, shape, 1)
    return (lane < 64) if head % 2 == 0 else (lane >= 64)


def _resident(arr):
    nd = arr.ndim
    return pl.BlockSpec(arr.shape, lambda *_: (0,) * nd, pipeline_mode=pl.Buffered(1))


def _params(n_axes):
    return pltpu.CompilerParams(dimension_semantics=("arbitrary",) * n_axes,
                                vmem_limit_bytes=VMEM_LIMIT)


def _prep_a_kernel(x_ref, g1_ref, w_ref, gq_ref, gkv_ref, wuq_ref, wukt_ref, cos_ref, sin_ref,
                   qa_ref, ka_ref, ckv_ref, kr_ref):
    xn = _rms(x_ref[...], g1_ref[...]).astype(BF16)
    z = jnp.dot(xn, w_ref[...], preferred_element_type=F32)
    aq = z[:, :A_Q_LORA]
    akv = z[:, A_Q_LORA:A_Q_LORA + A_KV_LORA]
    akr = z[:, A_Q_LORA + A_KV_LORA:]
    cos = cos_ref[...]
    sin = sin_ref[...]
    ckv = _rms(akv, gkv_ref[...])
    ckv_ref[...] = ckv
    kr = _rope_lanes(akr, cos, sin, A_ROPE // 2)
    kr_ref[...] = kr[:, :A_ROPE]
    ka_ref[...] = jnp.concatenate([ckv, kr], axis=1).astype(BF16)
    aqn = _rms(aq, gq_ref[...]).astype(BF16)
    qa = jnp.dot(aqn, wuq_ref[...], preferred_element_type=F32)
    for h in range(HEADS):
        pair = qa[:, (h // 2) * LANES:(h // 2 + 1) * LANES].astype(BF16)
        q_lat = jnp.dot(pair, wukt_ref[h], preferred_element_type=F32)
        q_rope = _rope_lanes(qa[:, HW + h * LANES:HW + (h + 1) * LANES], cos, sin, A_ROPE // 2)
        qa_ref[:, h * 256:(h + 1) * 256] = (
            jnp.concatenate([q_lat, q_rope], axis=1) * (A_SCALE * LOG2E)).astype(BF16)


def _prep_b_kernel(x_ref, g1_ref, w_ref, cos_ref, sin_ref, cosk_ref, sink_ref,
                   bq_ref, bk_ref, bv_ref, iq_ref, ik2_ref, iw_ref, bk32_ref, bv32_ref, ik32_ref):
    xn = _rms(x_ref[...], g1_ref[...]).astype(BF16)
    z = jnp.dot(xn, w_ref[...], preferred_element_type=F32)
    cos = cos_ref[...]
    sin = sin_ref[...]
    bq = _rope_lanes(z[:, 0:HW], cos, sin, B_ROT // 2)
    bq_ref[...] = (bq * (B_SCALE * LOG2E)).astype(BF16)
    bk = _rope_lanes(z[:, HW:2 * HW], cos, sin, B_ROT // 2)
    bk32_ref[...] = bk
    bk_ref[...] = bk.astype(BF16)
    bv = z[:, 2 * HW:3 * HW]
    bv32_ref[...] = bv
    bv_ref[...] = bv.astype(BF16)
    iq = _rope_lanes(z[:, 3 * HW:4 * HW], cos, sin, IDX_ROT // 2)
    iq_ref[...] = (iq * IDX_DIM ** -0.5).astype(BF16)
    ik = _rope_lanes(z[:, 4 * HW:4 * HW + LANES], cosk_ref[...], sink_ref[...], IDX_ROT // 2)
    ik32_ref[...] = ik[:, :IDX_DIM]
    lane = lax.broadcasted_iota(I32, ik.shape, 1)
    ik2_ref[...] = jnp.where(lane < IDX_DIM, ik, pltpu.roll(ik, IDX_DIM, 1)).astype(BF16)
    iw_ref[...] = z[:, 4 * HW + LANES:] * HEADS ** -0.5


def _prep_c_kernel(x_ref, g1_ref, w_ref, cq_ref, ck_ref, cv_ref):
    xn = _rms(x_ref[...], g1_ref[...]).astype(BF16)
    z = jnp.dot(xn, w_ref[...], preferred_element_type=F32)
    cq_ref[...] = (z[:, 0:HW] * (C_SCALE * LOG2E)).astype(BF16)
    ck_ref[...] = z[:, HW:2 * HW].astype(BF16)
    cv_ref[...] = z[:, 2 * HW:].astype(BF16)


def _prep_c32_kernel(x_ref, g1_ref, w_ref, ck_ref, cv_ref):
    xn = _rms(x_ref[...], g1_ref[...]).astype(BF16)
    z = jnp.dot(xn, w_ref[...], preferred_element_type=F32)
    ck_ref[...] = z[:, 0:HW]
    cv_ref[...] = z[:, HW:]


def _row_call(kernel, x, tm, operands, specs, outs, name):
    t = x.shape[0]
    name = f"{name}_t{t}"
    grid = (t // tm,)
    in_specs = [pl.BlockSpec((tm, x.shape[1]), lambda i: (i, 0))] + specs
    out_shape = [jax.ShapeDtypeStruct((t, w), dt) for w, dt in outs]
    out_specs = [pl.BlockSpec((tm, w), lambda i: (i, 0)) for w, _ in outs]
    return pl.pallas_call(kernel, grid=grid, in_specs=in_specs, out_specs=out_specs,
                          out_shape=out_shape, compiler_params=_params(1), name=name)(x, *operands)


def _table_spec(table, tm):
    nblk = table.shape[0] // tm
    return pl.BlockSpec((tm, table.shape[1]), lambda i: (i % nblk, 0))


def _attn_a_kernel(q_ref, k_ref, wuv_ref, o_ref, m_ref, acc_ref, *, tq, tk, q_pos0, l_valid):
    qi = pl.program_id(1)
    qs = jnp.concatenate([q_ref[:, h * 256:(h + 1) * 256] for h in range(HEADS)], axis=0)
    qpos0 = q_pos0 + qi * tq
    qrow = lax.rem(lax.broadcasted_iota(I32, (HEADS * tq, 1), 0), tq)
    limit_row = jnp.minimum(((qpos0 + qrow) // CHUNK + 1) * CHUNK, l_valid)
    limit_max = jnp.minimum(((qpos0 + tq - 1) // CHUNK + 1) * CHUNK, l_valid)
    nkb = (limit_max + tk - 1) // tk
    m_ref[...] = jnp.full(m_ref.shape, NEG_INF, F32)
    acc_ref[...] = jnp.zeros(acc_ref.shape, F32)
    v_ones = jnp.ones((tk, LANES), BF16)

    def scores(kb):
        start = pl.multiple_of(kb * tk, tk)
        return lax.dot_general(qs, k_ref[pl.ds(start, tk), :], (((1,), (1,)), ((), ())),
                               preferred_element_type=F32)

    def update(kb, s, masked):
        start = pl.multiple_of(kb * tk, tk)
        if masked:
            kpos = start + lax.broadcasted_iota(I32, (1, tk), 1)
            s = jnp.where(kpos < limit_row, s, NEG_INF)
        mx = s[:, 0:LANES]
        for j in range(1, tk // LANES):
            mx = jnp.maximum(mx, s[:, j * LANES:(j + 1) * LANES])
        m_old = m_ref[...]
        m_new = jnp.maximum(m_old, jnp.max(mx, axis=1, keepdims=True))
        alpha = jnp.exp2(m_old - m_new)
        p = jnp.exp2(s - m_new).astype(BF16)
        vext = jnp.concatenate([k_ref[pl.ds(start, tk), :A_KV_LORA], v_ones], axis=1)
        acc_ref[...] = alpha * acc_ref[...] + jnp.dot(p, vext, preferred_element_type=F32)
        m_ref[...] = m_new

    def body(kb, s):
        s_next = scores(kb + 1)
        update(kb, s, False)
        return s_next

    s_last = lax.fori_loop(0, nkb - 1, body, scores(0))
    update(nkb - 1, s_last, True)
    acc = acc_ref[...]
    o_lat = (acc[:, :A_KV_LORA] / acc[:, A_KV_LORA:]).astype(BF16)
    o = jnp.zeros((tq, HW), F32)
    for h in range(HEADS):
        o = o + jnp.dot(o_lat[h * tq:(h + 1) * tq], wuv_ref[h], preferred_element_type=F32)
    o_ref[...] = o.astype(BF16)


def _attn_a(qa, ka, wuv_pad, *, nb, s_q, l_keys, tq, tk, q_pos0, l_valid):
    nq = s_q // tq
    assert tq <= 2 * CHUNK <= tk and (tq <= CHUNK or q_pos0 % tq == 0)
    kern = functools.partial(_attn_a_kernel, tq=tq, tk=tk, q_pos0=q_pos0, l_valid=l_valid)
    return pl.pallas_call(
        kern, grid=(nb, nq),
        in_specs=[pl.BlockSpec((tq, HEADS * 256), lambda b, i: (b * nq + i, 0)),
                  pl.BlockSpec((l_keys, 256), lambda b, i: (b, 0), pipeline_mode=pl.Buffered(1)),
                  _resident(wuv_pad)],
        out_specs=pl.BlockSpec((tq, HW), lambda b, i: (b * nq + i, 0)),
        out_shape=jax.ShapeDtypeStruct((nb * s_q, HW), BF16),
        scratch_shapes=[pltpu.VMEM((HEADS * tq, 1), F32), pltpu.VMEM((HEADS * tq, 2 * LANES), F32)],
        compiler_params=_params(2), name=f"attn_a_q{s_q}")(qa, ka, wuv_pad)


def _sortable_key(x):
    bits = lax.bitcast_convert_type(x, I32)
    return bits ^ ((bits >> 31) & 0x7FFFFFFF)


def _attn_b_kernel(q_ref, iq_ref, iw_ref, k_ref, v_ref, ik_ref, o_ref, s_ref, m_ref, acc_ref,
                   *, tq, tk, q_pos0, l_valid, topk, idx_bits, group):
    qi = pl.program_id(1)
    qpos0 = q_pos0 + qi * tq
    row = lax.broadcasted_iota(I32, (tq, 1), 0)
    limit_row = jnp.minimum(((qpos0 + row) // CHUNK + 1) * CHUNK, l_valid)
    limit_max = jnp.minimum(((qpos0 + tq - 1) // CHUNK + 1) * CHUNK, l_valid)
    nkb = (limit_max + tk - 1) // tk
    kf = float(topk)

    iq = iq_ref[...]
    iqs = jnp.concatenate(
        [jnp.where(_head_half_mask((tq, LANES), h), iq[:, (h // 2) * LANES:(h // 2 + 1) * LANES],
                   jnp.zeros((tq, LANES), BF16)) for h in range(HEADS)], axis=0)
    w = iw_ref[...]

    def score_body(kb, carry):
        start = pl.multiple_of(kb * tk, tk)
        x = lax.dot_general(iqs, ik_ref[pl.ds(start, tk), :], (((1,), (1,)), ((), ())),
                            preferred_element_type=F32)
        sc = jnp.zeros((tq, tk), F32)
        for h in range(HEADS):
            sc = sc + jnp.maximum(x[h * tq:(h + 1) * tq], 0.0) * w[:, h:h + 1]
        kpos = start + lax.broadcasted_iota(I32, (1, tk), 1)
        sc = jnp.where(kpos < limit_row, sc + 0.0, NEG_INF)
        s_ref[kb] = _sortable_key(sc)
        return carry

    nkg = (nkb + group - 1) // group
    lax.fori_loop(0, nkg * group, score_body, 0)

    nsl = tk // LANES
    ones = jnp.ones((LANES, LANES), BF16)
    limit_rep = jnp.broadcast_to(limit_row, (tq, LANES))
    rh = min(tq, LANES)

    def count(pred):
        parts = []
        for r in range(tq // rh):
            rows = slice(r * rh, (r + 1) * rh)

            def body(g, c, rows=rows):
                for u in range(group):
                    kb = g * group + u
                    for j in range(nsl):
                        blk = s_ref[kb, rows, j * LANES:(j + 1) * LANES]
                        kpos = kb * tk + j * LANES + lax.broadcasted_iota(I32, (1, LANES), 1)
                        c = c + jnp.where(pred(blk, kpos, rows), 1.0, 0.0)
                return c

            parts.append(lax.fori_loop(0, nkg, body, jnp.zeros((rh, LANES), F32)))
        c = parts[0] if len(parts) == 1 else jnp.concatenate(parts, axis=0)
        return jnp.dot(c.astype(BF16), ones, preferred_element_type=F32)

    def row_max():
        parts = []
        for r in range(tq // rh):
            rows = slice(r * rh, (r + 1) * rh)

            def body(g, mx, rows=rows):
                for u in range(group):
                    for j in range(nsl):
                        mx = jnp.maximum(mx, s_ref[g * group + u, rows, j * LANES:(j + 1) * LANES])
                return mx

            parts.append(lax.fori_loop(0, nkg, body, jnp.full((rh, LANES), INT_MIN, I32)))
        mx = parts[0] if len(parts) == 1 else jnp.concatenate(parts, axis=0)
        return jnp.broadcast_to(jnp.max(mx, axis=1, keepdims=True), (tq, LANES))

    key_max = row_max()
    n_all = (nkg * group * tk).astype(F32)
    n_pos = count(lambda blk, kpos, rows: blk >= 0)
    thr0 = jnp.where(n_pos >= kf, 0, INT_MIN).astype(I32)
    n_thr0 = jnp.where(n_pos >= kf, n_pos, n_all)

    def all_rows(flag):
        return jnp.min(jnp.where(flag, 1.0, 0.0)) > 0.5

    def bit_cond(state):
        i, resolved, _, _ = state
        return (i < 31) & (resolved == 0)

    def bit_body(state):
        i, _, t, n_t = state
        cand = t + (jnp.int32(1) << (30 - i))
        c = lax.cond(all_rows(cand > key_max), lambda: jnp.zeros((tq, LANES), F32),
                     lambda: count(lambda blk, kpos, rows: blk >= cand[rows]))
        take = c >= kf
        n_t = jnp.where(take, c, n_t)
        return i + 1, all_rows(n_t == kf).astype(I32), jnp.where(take, cand, t), n_t

    _, _, thr, n_ge = lax.while_loop(
        bit_cond, bit_body, (jnp.int32(0), all_rows(n_thr0 == kf).astype(I32), thr0, n_thr0))
    excess = jnp.max(jnp.where(n_ge > kf, 1.0, 0.0)) > 0.0

    def tie_cut(_):
        need = kf - count(lambda blk, kpos, rows: blk > thr[rows])

        def jb(i, j):
            cand = j + (jnp.int32(1) << (idx_bits - 1 - i))
            c = count(lambda blk, kpos, rows: (blk == thr[rows]) & (kpos < cand[rows]))
            return jnp.where(c < need, cand, j)
        return lax.fori_loop(0, idx_bits, jb, jnp.zeros((tq, LANES), I32))

    cut = lax.cond(excess, tie_cut, lambda _: jnp.full((tq, LANES), 2 ** 30, I32), 0)

    def bias_body(kb, carry):
        for j in range(nsl):
            blk = s_ref[kb, :, j * LANES:(j + 1) * LANES]
            kpos = kb * tk + j * LANES + lax.broadcasted_iota(I32, (1, LANES), 1)
            sel = ((blk > thr) | ((blk == thr) & (kpos <= cut))) & (kpos < limit_rep)
            s_ref[kb, :, j * LANES:(j + 1) * LANES] = lax.bitcast_convert_type(
                jnp.where(sel, 0.0, NEG_INF).astype(F32), I32)
        return carry

    lax.fori_loop(0, nkb, bias_body, 0)

    m_ref[...] = jnp.full(m_ref.shape, NEG_INF, F32)
    acc_ref[...] = jnp.zeros(acc_ref.shape, F32)
    v_ones = jnp.ones((tk, LANES), BF16)

    def attn_body(kb, carry):
        start = pl.multiple_of(kb * tk, tk)
        bias = lax.bitcast_convert_type(s_ref[kb], F32)
        for pr in range(HEADS // 2):
            kblk = k_ref[pl.ds(start, tk), pr * LANES:(pr + 1) * LANES]
            vext = jnp.concatenate([v_ref[pl.ds(start, tk), pr * LANES:(pr + 1) * LANES], v_ones], axis=1)
            qpair = q_ref[:, pr * LANES:(pr + 1) * LANES]
            for h in (2 * pr, 2 * pr + 1):
                qm = jnp.where(_head_half_mask((tq, LANES), h), qpair, jnp.zeros((tq, LANES), BF16))
                s = lax.dot_general(qm, kblk, (((1,), (1,)), ((), ())), preferred_element_type=F32) + bias
                mx = s[:, 0:LANES]
                for j in range(1, nsl):
                    mx = jnp.maximum(mx, s[:, j * LANES:(j + 1) * LANES])
                m_old = m_ref[h]
                m_new = jnp.maximum(m_old, jnp.max(mx, axis=1, keepdims=True))
                m_safe = jnp.where(m_new == NEG_INF, 0.0, m_new)
                alpha = jnp.exp2(m_old - m_safe)
                p = jnp.exp2(s - m_safe).astype(BF16)
                acc_ref[h] = alpha * acc_ref[h] + jnp.dot(p, vext, preferred_element_type=F32)
                m_ref[h] = m_new
        return carry

    lax.fori_loop(0, nkb, attn_body, 0)
    lane = lax.broadcasted_iota(I32, (tq, LANES), 1)
    for pr in range(HEADS // 2):
        a_even, a_odd = acc_ref[2 * pr], acc_ref[2 * pr + 1]
        o_even = a_even[:, :LANES] / a_even[:, LANES:]
        o_odd = a_odd[:, :LANES] / a_odd[:, LANES:]
        o_ref[:, pr * LANES:(pr + 1) * LANES] = jnp.where(lane < 64, o_even, o_odd).astype(BF16)


def _attn_b(bq, iq, iw, bk, bv, ik2, *, nb, s_q, l_keys, tq, tk, q_pos0, l_valid, topk):
    nq = s_q // tq
    nkb_max = l_keys // tk
    idx_bits = max(1, int(l_keys - 1).bit_length())
    assert l_keys // LANES <= 256, "per-lane key counts must stay exact in bf16"
    group = 2 if nkb_max % 2 == 0 else 1
    kern = functools.partial(_attn_b_kernel, tq=tq, tk=tk, q_pos0=q_pos0, l_valid=l_valid,
                             topk=topk, idx_bits=idx_bits, group=group)
    qmap = lambda b, i: (b * nq + i, 0)
    kmap = lambda b, i: (b, 0)
    return pl.pallas_call(
        kern, grid=(nb, nq),
        in_specs=[pl.BlockSpec((tq, HW), qmap), pl.BlockSpec((tq, HW), qmap),
                  pl.BlockSpec((tq, LANES), qmap),
                  pl.BlockSpec((l_keys, HW), kmap, pipeline_mode=pl.Buffered(1)),
                  pl.BlockSpec((l_keys, HW), kmap, pipeline_mode=pl.Buffered(1)),
                  pl.BlockSpec((l_keys, LANES), kmap, pipeline_mode=pl.Buffered(1))],
        out_specs=pl.BlockSpec((tq, HW), qmap),
        out_shape=jax.ShapeDtypeStruct((nb * s_q, HW), BF16),
        scratch_shapes=[pltpu.VMEM((nkb_max, tq, tk), I32),
                        pltpu.VMEM((HEADS, tq, 1), F32), pltpu.VMEM((HEADS, tq, 2 * LANES), F32)],
        compiler_params=_params(2), name=f"attn_b_q{s_q}")(bq, iq, iw, bk, bv, ik2)


def _attn_c_kernel(*refs, tq, off):
    q_ref = refs[0]
    k_refs = refs[1:1 + C_BAND_BLOCKS]
    v_refs = refs[1 + C_BAND_BLOCKS:1 + 2 * C_BAND_BLOCKS]
    bias_ref = refs[1 + 2 * C_BAND_BLOCKS]
    o_ref = refs[2 + 2 * C_BAND_BLOCKS]
    kc_ref, vc_ref = refs[3 + 2 * C_BAND_BLOCKS:]
    qi = pl.program_id(1)
    for j in range(C_BAND_BLOCKS):
        kc_ref[j * C_KEY_BLOCK:(j + 1) * C_KEY_BLOCK, :] = k_refs[j][...]
        vc_ref[j * C_KEY_BLOCK:(j + 1) * C_KEY_BLOCK, :] = v_refs[j][...]
    col = lax.broadcasted_iota(I32, (1, C_BAND), 1)
    col_valid = col >= (off - qi) * C_KEY_BLOCK
    q = q_ref[...]
    o_even = None
    for h in range(HEADS):
        pr = h // 2
        qm = jnp.where(_head_half_mask((tq, LANES), h), q[:, pr * LANES:(pr + 1) * LANES],
                       jnp.zeros((tq, LANES), BF16))
        s = lax.dot_general(qm, kc_ref[:, pr * LANES:(pr + 1) * LANES], (((1,), (1,)), ((), ())),
                            preferred_element_type=F32) + bias_ref[h]
        s = jnp.where(col_valid, s, NEG_INF)
        m = jnp.max(s, axis=1, keepdims=True)
        p = jnp.exp2(s - m)
        l = jnp.sum(p, axis=1, keepdims=True)
        o_h = jnp.dot(p.astype(BF16), vc_ref[:, pr * LANES:(pr + 1) * LANES],
                      preferred_element_type=F32) / l
        if h % 2 == 0:
            o_even = o_h
        else:
            lane = lax.broadcasted_iota(I32, (tq, LANES), 1)
            o_ref[:, pr * LANES:(pr + 1) * LANES] = jnp.where(lane < 64, o_even, o_h).astype(BF16)


def _attn_c(cq, ck, cv, bias, *, nb, s_q, l_keys, tq, off):
    nq = s_q // tq
    nkblk = l_keys // C_KEY_BLOCK
    kern = functools.partial(_attn_c_kernel, tq=tq, off=off)
    qmap = lambda b, i: (b * nq + i, 0)

    def kspec(j):
        return pl.BlockSpec((C_KEY_BLOCK, HW), lambda b, i: (b * nkblk + jnp.maximum(i + j - off, 0), 0))

    kv_specs = [kspec(j) for j in range(C_BAND_BLOCKS)]
    return pl.pallas_call(
        kern, grid=(nb, nq),
        in_specs=[pl.BlockSpec((tq, HW), qmap)] + kv_specs + kv_specs + [_resident(bias)],
        out_specs=pl.BlockSpec((tq, HW), qmap),
        out_shape=jax.ShapeDtypeStruct((nb * s_q, HW), BF16),
        scratch_shapes=[pltpu.VMEM((C_BAND, HW), BF16), pltpu.VMEM((C_BAND, HW), BF16)],
        compiler_params=_params(2), name=f"attn_c_q{s_q}")(
            cq, *([ck] * C_BAND_BLOCKS), *([cv] * C_BAND_BLOCKS), bias)


def _merge_kernel(x_ref, g1_ref, oa_ref, ob_ref, oc_ref, wg_ref, woa_ref, wob_ref, woc_ref, wout_ref, y_ref):
    x = x_ref[...]
    xn = _rms(x, g1_ref[...]).astype(BF16)
    merged = jnp.zeros(x.shape, F32)
    for b, (o_ref, w_ref) in enumerate(((oa_ref, woa_ref), (ob_ref, wob_ref), (oc_ref, woc_ref))):
        gate = jax.nn.sigmoid(jnp.dot(xn, wg_ref[:, b * D_MODEL:(b + 1) * D_MODEL],
                                      preferred_element_type=F32))
        merged = merged + gate * jnp.dot(o_ref[...], w_ref[...], preferred_element_type=F32)
    y_ref[...] = x + jnp.dot(merged.astype(BF16), wout_ref[...], preferred_element_type=F32)


def _mlp_kernel(x_ref, g2_ref, wup_ref, wdn_ref, gf_ref, y_ref, *, ff_chunk, final):
    x = x_ref[...]
    xn = _rms(x, g2_ref[...]).astype(BF16)
    acc = jnp.zeros(x.shape, F32)
    for c in range(D_FF // ff_chunk):
        h = jnp.dot(xn, wup_ref[:, c * ff_chunk:(c + 1) * ff_chunk], preferred_element_type=F32)
        h = jnp.square(jnp.maximum(h, 0.0)).astype(BF16)
        acc = acc + jnp.dot(h, wdn_ref[c * ff_chunk:(c + 1) * ff_chunk, :], preferred_element_type=F32)
    y = x + acc
    if final:
        y = _rms(y, gf_ref[...])
    y_ref[...] = y


def _rope_tables(pos, rot, period, live_lanes):
    half = rot // 2
    inv_freq = ROPE_THETA ** (-jnp.arange(half, dtype=F32) / half)
    ang = pos.astype(F32)[:, None] * inv_freq[None, :]
    cos, sin = jnp.cos(ang), jnp.sin(ang)
    lane = np.arange(LANES)
    r = lane % period
    live = (lane < live_lanes) & (r < rot)
    fi = np.where(live, r % half, 0)
    sign = np.where(r < half, -1.0, 1.0).astype(np.float32)
    cos_t = jnp.where(live[None, :], cos[:, fi], 1.0)
    sin_t = jnp.where(live[None, :], sin[:, fi] * sign[None, :], 0.0)
    return cos_t, sin_t


def _layer_weights(w_in, a_w_uq, a_w_uk, a_w_uv):
    offs = np.cumsum([0, A_Q_LORA, A_KV_LORA, A_ROPE, HW, HW, HW, HW, IDX_DIM, HEADS, HW, HW, HW,
                      3 * D_MODEL])
    seg = lambda i: w_in[:, offs[i]:offs[i + 1]]
    zeros = lambda n: jnp.zeros((D_MODEL, n), w_in.dtype)
    w_a = jnp.concatenate([seg(0), seg(1), seg(2), zeros(LANES - A_ROPE)], axis=1)
    w_b = jnp.concatenate([seg(3), seg(4), seg(5), seg(6), seg(7), zeros(LANES - IDX_DIM),
                           seg(8), zeros(LANES - HEADS)], axis=1)
    w_c = jnp.concatenate([seg(9), seg(10), seg(11)], axis=1)
    w_g = seg(12)
    uq = a_w_uq.reshape(A_Q_LORA, HEADS, A_NOPE + A_ROPE)
    uq_nope = uq[:, :, :A_NOPE].reshape(A_Q_LORA, HW)
    uq_rope = jnp.pad(uq[:, :, A_NOPE:], ((0, 0), (0, 0), (0, LANES - A_ROPE))).reshape(A_Q_LORA, HEADS * LANES)
    w_uq = jnp.concatenate([uq_nope, uq_rope], axis=1)
    uk = a_w_uk.reshape(A_KV_LORA, HEADS, A_NOPE)
    ukt = jnp.transpose(uk, (1, 2, 0))
    ukt_pad = jnp.stack([jnp.pad(ukt[h], ((0, 64), (0, 0)) if h % 2 == 0 else ((64, 0), (0, 0)))
                         for h in range(HEADS)], axis=0)
    uv = a_w_uv.reshape(A_KV_LORA, HEADS, A_VDIM)
    uv_pad = jnp.stack([jnp.pad(uv[:, h], ((0, 0), (h * A_VDIM, HW - (h + 1) * A_VDIM)))
                        for h in range(HEADS)], axis=0)
    bf = lambda a: a.astype(BF16)
    return bf(w_a), bf(w_b), bf(w_c), bf(w_g), bf(w_uq), bf(ukt_pad), bf(uv_pad)


def _band_bias(rel_bias, q_pos, k_pos, k_live):
    q_c, k_c = q_pos // CHUNK, k_pos // CHUNK
    vis = (k_c[None, :] <= q_c[:, None]) & (k_c[None, :] >= q_c[:, None] - C_LEFT_CHUNKS) & k_live[None, :]
    idx = np.clip(q_pos[:, None] - k_pos[None, :], -C_REL_CLIP, C_REL_CLIP) + C_REL_CLIP
    bias = rel_bias[:, idx].astype(F32) * LOG2E
    return jnp.where(vis[None], bias, NEG_INF)


def _pad_rows(a, n):
    return jnp.pad(a, ((0, 0), (0, n - a.shape[1]), (0, 0)))


def _mixers(x2d, lw, *, nb, s_q, q_pos0, past, tm, tq_a, tq_b, tq_c, tk_a, tk_b, tables, c_rel_bias):
    (g1, gq, gkv, w_a, w_b, w_c, w_uq, ukt_pad, uv_pad) = lw
    cos_a, sin_a, cos_b, sin_b, cos_k, sin_k = tables
    t = nb * s_q
    res = _resident
    qa, ka, ckv32, kr32 = _row_call(
        _prep_a_kernel, x2d, tm, (g1, w_a, gq, gkv, w_uq, ukt_pad, cos_a, sin_a),
        [res(g1), res(w_a), res(gq), res(gkv), res(w_uq), res(ukt_pad), _table_spec(cos_a, tm),
         _table_spec(sin_a, tm)],
        [(HEADS * 256, BF16), (256, BF16), (A_KV_LORA, F32), (A_ROPE, F32)], "prep_a")
    bq, bk, bv, iq, ik2, iw, bk32, bv32, ik32 = _row_call(
        _prep_b_kernel, x2d, tm, (g1, w_b, cos_b, sin_b, cos_k, sin_k),
        [res(g1), res(w_b), _table_spec(cos_b, tm), _table_spec(sin_b, tm), _table_spec(cos_k, tm),
         _table_spec(sin_k, tm)],
        [(HW, BF16), (HW, BF16), (HW, BF16), (HW, BF16), (LANES, BF16), (LANES, F32),
         (HW, F32), (HW, F32), (IDX_DIM, F32)], "prep_b")
    cq, ck, cv = _row_call(_prep_c_kernel, x2d, tm, (g1, w_c), [res(g1), res(w_c)],
                           [(HW, BF16), (HW, BF16), (HW, BF16)], "prep_c")

    if past is None:
        l_valid = s_q
        l_keys = s_q
        keys_a, keys_bk, keys_bv, keys_ik = ka, bk, bv, ik2
        keys_ck, keys_cv = ck, cv
        lc_keys = s_q
        c_off = C_BAND_BLOCKS - 1
        qp = np.arange(tq_c)
        kp = np.arange(C_BAND) - C_REACH
        bias = _band_bias(c_rel_bias, qp, kp, np.ones(C_BAND, bool))
        keep = min(C_REACH, s_q)
        x_tail = x2d.reshape(nb, s_q, D_MODEL)[:, s_q - keep:].reshape(nb * keep, D_MODEL)
        w_ckv = w_c[:, HW:]
        ck32, cv32 = _row_call(_prep_c32_kernel, x_tail, min(tm, nb * keep), (g1, w_ckv),
                               [res(g1), res(w_ckv)], [(HW, F32), (HW, F32)], "prep_c32")
        ck32 = ck32.reshape(nb, keep, HEADS, C_HEAD_DIM)
        cv32 = cv32.reshape(nb, keep, HEADS, C_HEAD_DIM)
    else:
        p_ckv, p_kr, p_bk, p_bv, p_ik, p_ck, p_cv = past
        past_len = p_ckv.shape[1]
        l_valid = past_len + s_q
        l_keys = -(-l_valid // LANES) * LANES
        r3 = lambda a: a.reshape(nb, s_q, a.shape[-1])
        cat = lambda old, new: _pad_rows(jnp.concatenate([old.astype(BF16), r3(new)], axis=1), l_keys)
        flat = lambda a: a.reshape(nb * a.shape[1], a.shape[2])
        old_a = jnp.concatenate([p_ckv, p_kr, jnp.zeros((nb, past_len, 256 - A_KV_LORA - A_ROPE), F32)], axis=-1)
        keys_a = flat(cat(old_a, ka))
        keys_bk = flat(cat(p_bk.reshape(nb, past_len, HW), bk))
        keys_bv = flat(cat(p_bv.reshape(nb, past_len, HW), bv))
        keys_ik = flat(cat(jnp.concatenate([p_ik, p_ik], axis=-1), ik2))
        w_c_len = p_ck.shape[1]
        lc_valid = w_c_len + s_q
        lc_keys = C_BAND
        catc = lambda old, new: _pad_rows(jnp.concatenate([old.astype(BF16), r3(new)], axis=1), lc_keys)
        keys_ck = flat(catc(p_ck.reshape(nb, w_c_len, HW), ck))
        keys_cv = flat(catc(p_cv.reshape(nb, w_c_len, HW), cv))
        c_off = 0
        qp = np.arange(q_pos0, q_pos0 + s_q)
        kp = np.arange(past_len - w_c_len, past_len - w_c_len + lc_keys)
        bias = _band_bias(c_rel_bias, qp, kp, np.arange(lc_keys) < lc_valid)
        ck32, cv32 = _row_call(_prep_c32_kernel, x2d, tm, (g1, w_c[:, HW:]),
                               [res(g1), res(w_c[:, HW:])], [(HW, F32), (HW, F32)], "prep_c32")
        ck32 = ck32.reshape(nb, s_q, HEADS, C_HEAD_DIM)
        cv32 = cv32.reshape(nb, s_q, HEADS, C_HEAD_DIM)

    topk = min(B_TOPK_MAX, l_valid // 4)
    tk_a = min(tk_a, l_keys)
    tk_b = min(tk_b, l_keys)
    o_a = _attn_a(qa, keys_a, uv_pad, nb=nb, s_q=s_q, l_keys=l_keys, tq=tq_a, tk=tk_a,
                  q_pos0=q_pos0, l_valid=l_valid)
    o_b = _attn_b(bq, iq, iw, keys_bk, keys_bv, keys_ik, nb=nb, s_q=s_q, l_keys=l_keys, tq=tq_b,
                  tk=tk_b, q_pos0=q_pos0, l_valid=l_valid, topk=topk)
    o_c = _attn_c(cq, keys_ck, keys_cv, bias, nb=nb, s_q=s_q, l_keys=lc_keys, tq=tq_c, off=c_off)
    new_rows = (ckv32.reshape(nb, s_q, A_KV_LORA), kr32.reshape(nb, s_q, A_ROPE),
                bk32.reshape(nb, s_q, HEADS, B_HEAD_DIM), bv32.reshape(nb, s_q, HEADS, B_HEAD_DIM),
                ik32.reshape(nb, s_q, IDX_DIM), ck32, cv32)
    return o_a, o_b, o_c, new_rows


def _layer(x2d, lw_mix, lw_rest, gf, *, final, tm, **kw):
    o_a, o_b, o_c, new_rows = _mixers(x2d, lw_mix, tm=tm, **kw)
    g1 = lw_mix[0]
    (w_g, w_oa, w_ob, w_oc, w_out, g2, w_up, w_dn) = lw_rest
    res = _resident
    row = lambda w: pl.BlockSpec((tm, w), lambda i: (i, 0))
    (x1,) = _row_call(_merge_kernel, x2d, tm, (g1, o_a, o_b, o_c, w_g, w_oa, w_ob, w_oc, w_out),
                      [res(g1), row(HW), row(HW), row(HW), res(w_g), res(w_oa), res(w_ob), res(w_oc),
                       res(w_out)], [(D_MODEL, F32)], "merge")
    (x2,) = _row_call(functools.partial(_mlp_kernel, ff_chunk=1024, final=final), x1, tm,
                      (g2, w_up, w_dn, gf), [res(g2), res(w_up), res(w_dn), res(gf)], [(D_MODEL, F32)],
                      "mlp_final" if final else "mlp")
    return x2, new_rows


def kernel(x_prompt, x_sample, cache_a_ckv, cache_a_krope, cache_b_k, cache_b_v, cache_b_idx_k, cache_c_k, cache_c_v, norm1, w_in, a_q_norm, a_kv_norm, a_w_uq, a_w_uk, a_w_uv, c_rel_bias, w_oa, w_ob, w_oc, w_out, norm2, w_up, w_down, final_norm):
    depth = w_in.shape[0]
    nb_p, s_p, _ = x_prompt.shape
    nb_s, s_s, _ = x_sample.shape
    past_len = cache_a_ckv.shape[2]
    assert s_p % 256 == 0 and CHUNK % s_s == 0 and past_len % CHUNK == 0
    tm_p = 512 if (nb_p * s_p) % 512 == 0 else 256
    tm_s = nb_s * s_s

    pos_p = jnp.arange(s_p, dtype=jnp.int32)
    pos_s = jnp.tile(jnp.arange(past_len, past_len + s_s, dtype=jnp.int32), nb_s)

    def tables(pos):
        return (_rope_tables(pos, A_ROPE, A_ROPE, A_ROPE) + _rope_tables(pos, B_ROT, B_HEAD_DIM, LANES)
                + _rope_tables(pos, IDX_ROT, IDX_DIM, IDX_DIM))

    tab_p, tab_s = tables(pos_p), tables(pos_s)
    gf = final_norm.reshape(1, D_MODEL)
    bf = lambda a: a.astype(BF16)

    xp = x_prompt.reshape(nb_p * s_p, D_MODEL)
    xs = x_sample.reshape(nb_s * s_s, D_MODEL)
    rows_p, rows_s = [], []
    for l in range(depth):
        w_a, w_b, w_c, w_g, w_uq, ukt_pad, uv_pad = _layer_weights(w_in[l], a_w_uq[l], a_w_uk[l], a_w_uv[l])
        lw_mix = (norm1[l].reshape(1, -1), a_q_norm[l].reshape(1, -1), a_kv_norm[l].reshape(1, -1),
                  w_a, w_b, w_c, w_uq, ukt_pad, uv_pad)
        lw_rest = (w_g, bf(w_oa[l]), bf(w_ob[l]), bf(w_oc[l]), bf(w_out[l]), norm2[l].reshape(1, -1),
                   bf(w_up[l]), bf(w_down[l]))
        final = l == depth - 1
        past = (cache_a_ckv[l], cache_a_krope[l], cache_b_k[l], cache_b_v[l], cache_b_idx_k[l],
                cache_c_k[l], cache_c_v[l])
        xp, new_p = _layer(xp, lw_mix, lw_rest, gf, final=final, tm=tm_p, nb=nb_p, s_q=s_p, q_pos0=0,
                           past=None, tq_a=128, tq_b=256, tq_c=128, tk_a=512, tk_b=512,
                           tables=tab_p, c_rel_bias=c_rel_bias[l])
        xs, new_s = _layer(xs, lw_mix, lw_rest, gf, final=final, tm=tm_s, nb=nb_s, s_q=s_s,
                           q_pos0=past_len, past=past, tq_a=s_s, tq_b=s_s, tq_c=s_s, tk_a=1 << 30,
                           tk_b=1 << 30, tables=tab_s, c_rel_bias=c_rel_bias[l])
        rows_p.append(new_p)
        rows_s.append(new_s)

    y_prompt = xp.reshape(nb_p, s_p, D_MODEL)
    y_sample = xs.reshape(nb_s, s_s, D_MODEL)
    stack = lambda rows, i: jnp.stack([r[i] for r in rows], axis=0)
    outs = [y_prompt, y_sample]
    for i in range(7):
        outs += [stack(rows_p, i), stack(rows_s, i)]
    return tuple(outs)
```

```python
import functools

import numpy as np
import jax
import jax.numpy as jnp
from jax import lax
from jax.experimental import pallas as pl
from jax.experimental.pallas import tpu as pltpu

F32 = jnp.float32
BF16 = jnp.bfloat16
I32 = jnp.int32

D_MODEL = 1024
CHUNK = 64
ROPE_THETA = 500000.0
RMS_EPS = 1e-6
HEADS = 8
A_Q_LORA = 256
A_KV_LORA = 128
A_NOPE = 64
A_ROPE = 32
A_VDIM = 64
A_SCALE = (A_NOPE + A_ROPE) ** -0.5
B_HEAD_DIM = 64
B_ROT = B_HEAD_DIM // 4
B_SCALE = B_HEAD_DIM ** -0.5
B_TOPK_MAX = 256
IDX_DIM = 64
IDX_ROT = IDX_DIM // 4
C_HEAD_DIM = 64
C_SCALE = C_HEAD_DIM ** -0.5
C_LEFT_CHUNKS = 8
C_REACH = C_LEFT_CHUNKS * CHUNK
C_REL_CLIP = 128
D_FF = 4 * D_MODEL
HW = HEADS * 64
LANES = 128
C_KEY_BLOCK = 128
C_BAND_BLOCKS = 6
C_BAND = C_KEY_BLOCK * C_BAND_BLOCKS
VMEM_LIMIT = 56 * 1024 * 1024
NEG_INF = float("-inf")
LOG2E = 1.4426950408889634
INT_MIN = -2 ** 31


def _rms(x, g):
    ms = jnp.mean(x * x, axis=-1, keepdims=True)
    return x * lax.rsqrt(ms + RMS_EPS) * g


def _rope_lanes(x, cos, sin, half):
    w = x.shape[-1]
    outs = []
    for j in range(w // LANES):
        xs = x[:, j * LANES:(j + 1) * LANES]
        lane = lax.broadcasted_iota(I32, xs.shape, 1)
        first = (lane % (2 * half)) < half
        partner = jnp.where(first, pltpu.roll(xs, LANES - half, 1), pltpu.roll(xs, half, 1))
        outs.append(xs * cos + partner * sin)
    return outs[0] if len(outs) == 1 else jnp.concatenate(outs, axis=1)


def _head_half_mask(shape, head):
    lane = lax.broadcasted_iota(I32, shape, 1)
    return (lane < 64) if head % 2 == 0 else (lane >= 64)


def _resident(arr):
    nd = arr.ndim
    return pl.BlockSpec(arr.shape, lambda *_: (0,) * nd, pipeline_mode=pl.Buffered(1))


def _params(n_axes):
    return pltpu.CompilerParams(dimension_semantics=("arbitrary",) * n_axes,
                                vmem_limit_bytes=VMEM_LIMIT)


def _prep_a_kernel(x_ref, g1_ref, w_ref, gq_ref, gkv_ref, wuq_ref, wukt_ref, cos_ref, sin_ref,
                   qa_ref, ka_ref, ckv_ref, kr_ref):
    xn = _rms(x_ref[...], g1_ref[...]).astype(BF16)
    z = jnp.dot(xn, w_ref[...], preferred_element_type=F32)
    aq = z[:, :A_Q_LORA]
    akv = z[:, A_Q_LORA:A_Q_LORA + A_KV_LORA]
    akr = z[:, A_Q_LORA + A_KV_LORA:]
    cos = cos_ref[...]
    sin = sin_ref[...]
    ckv = _rms(akv, gkv_ref[...])
    ckv_ref[...] = ckv
    kr = _rope_lanes(akr, cos, sin, A_ROPE // 2)
    kr_ref[...] = kr[:, :A_ROPE]
    ka_ref[...] = jnp.concatenate([ckv, kr], axis=1).astype(BF16)
    aqn = _rms(aq, gq_ref[...]).astype(BF16)
    qa = jnp.dot(aqn, wuq_ref[...], preferred_element_type=F32)
    for h in range(HEADS):
        pair = qa[:, (h // 2) * LANES:(h // 2 + 1) * LANES].astype(BF16)
        q_lat = jnp.dot(pair, wukt_ref[h], preferred_element_type=F32)
        q_rope = _rope_lanes(qa[:, HW + h * LANES:HW + (h + 1) * LANES], cos, sin, A_ROPE // 2)
        qa_ref[:, h * 256:(h + 1) * 256] = (
            jnp.concatenate([q_lat, q_rope], axis=1) * (A_SCALE * LOG2E)).astype(BF16)


def _prep_b_kernel(x_ref, g1_ref, w_ref, cos_ref, sin_ref, cosk_ref, sink_ref,
                   bq_ref, bk_ref, bv_ref, iq_ref, ik2_ref, iw_ref, bk32_ref, bv32_ref, ik32_ref):
    xn = _rms(x_ref[...], g1_ref[...]).astype(BF16)
    z = jnp.dot(xn, w_ref[...], preferred_element_type=F32)
    cos = cos_ref[...]
    sin = sin_ref[...]
    bq = _rope_lanes(z[:, 0:HW], cos, sin, B_ROT // 2)
    bq_ref[...] = (bq * (B_SCALE * LOG2E)).astype(BF16)
    bk = _rope_lanes(z[:, HW:2 * HW], cos, sin, B_ROT // 2)
    bk32_ref[...] = bk
    bk_ref[...] = bk.astype(BF16)
    bv = z[:, 2 * HW:3 * HW]
    bv32_ref[...] = bv
    bv_ref[...] = bv.astype(BF16)
    iq = _rope_lanes(z[:, 3 * HW:4 * HW], cos, sin, IDX_ROT // 2)
    iq_ref[...] = (iq * IDX_DIM ** -0.5).astype(BF16)
    ik = _rope_lanes(z[:, 4 * HW:4 * HW + LANES], cosk_ref[...], sink_ref[...], IDX_ROT // 2)
    ik32_ref[...] = ik[:, :IDX_DIM]
    lane = lax.broadcasted_iota(I32, ik.shape, 1)
    ik2_ref[...] = jnp.where(lane < IDX_DIM, ik, pltpu.roll(ik, IDX_DIM, 1)).astype(BF16)
    iw_ref[...] = z[:, 4 * HW + LANES:] * HEADS ** -0.5


def _prep_c_kernel(x_ref, g1_ref, w_ref, cq_ref, ck_ref, cv_ref):
    xn = _rms(x_ref[...], g1_ref[...]).astype(BF16)
    z = jnp.dot(xn, w_ref[...], preferred_element_type=F32)
    cq_ref[...] = (z[:, 0:HW] * (C_SCALE * LOG2E)).astype(BF16)
    ck_ref[...] = z[:, HW:2 * HW].astype(BF16)
    cv_ref[...] = z[:, 2 * HW:].astype(BF16)


def _prep_c32_kernel(x_ref, g1_ref, w_ref, ck_ref, cv_ref):
    xn = _rms(x_ref[...], g1_ref[...]).astype(BF16)
    z = jnp.dot(xn, w_ref[...], preferred_element_type=F32)
    ck_ref[...] = z[:, 0:HW]
    cv_ref[...] = z[:, HW:]


def _row_call(kernel, x, tm, operands, specs, outs, name):
    t = x.shape[0]
    name = f"{name}_t{t}"
    grid = (t // tm,)
    in_specs = [pl.BlockSpec((tm, x.shape[1]), lambda i: (i, 0))] + specs
    out_shape = [jax.ShapeDtypeStruct((t, w), dt) for w, dt in outs]
    out_specs = [pl.BlockSpec((tm, w), lambda i: (i, 0)) for w, _ in outs]
    return pl.pallas_call(kernel, grid=grid, in_specs=in_specs, out_specs=out_specs,
                          out_shape=out_shape, compiler_params=_params(1), name=name)(x, *operands)


def _table_spec(table, tm):
    nblk = table.shape[0] // tm
    return pl.BlockSpec((tm, table.shape[1]), lambda i: (i % nblk, 0))


def _attn_a_kernel(q_ref, k_ref, wuv_ref, o_ref, m_ref, acc_ref, *, tq, tk, q_pos0, l_valid):
    qi = pl.program_id(1)
    qs = jnp.concatenate([q_ref[:, h * 256:(h + 1) * 256] for h in range(HEADS)], axis=0)
    qpos0 = q_pos0 + qi * tq
    qrow = lax.rem(lax.broadcasted_iota(I32, (HEADS * tq, 1), 0), tq)
    limit_row = jnp.minimum(((qpos0 + qrow) // CHUNK + 1) * CHUNK, l_valid)
    limit_max = jnp.minimum(((qpos0 + tq - 1) // CHUNK + 1) * CHUNK, l_valid)
    nkb = (limit_max + tk - 1) // tk
    m_ref[...] = jnp.full(m_ref.shape, NEG_INF, F32)
    acc_ref[...] = jnp.zeros(acc_ref.shape, F32)
    v_ones = jnp.ones((tk, LANES), BF16)

    def scores(kb):
        start = pl.multiple_of(kb * tk, tk)
        return lax.dot_general(qs, k_ref[pl.ds(start, tk), :], (((1,), (1,)), ((), ())),
                               preferred_element_type=F32)

    def update(kb, s, masked):
        start = pl.multiple_of(kb * tk, tk)
        if masked:
            kpos = start + lax.broadcasted_iota(I32, (1, tk), 1)
            s = jnp.where(kpos < limit_row, s, NEG_INF)
        mx = s[:, 0:LANES]
        for j in range(1, tk // LANES):
            mx = jnp.maximum(mx, s[:, j * LANES:(j + 1) * LANES])
        m_old = m_ref[...]
        m_new = jnp.maximum(m_old, jnp.max(mx, axis=1, keepdims=True))
        alpha = jnp.exp2(m_old - m_new)
        p = jnp.exp2(s - m_new).astype(BF16)
        vext = jnp.concatenate([k_ref[pl.ds(start, tk), :A_KV_LORA], v_ones], axis=1)
        acc_ref[...] = alpha * acc_ref[...] + jnp.dot(p, vext, preferred_element_type=F32)
        m_ref[...] = m_new

    def body(kb, s):
        s_next = scores(kb + 1)
        update(kb, s, False)
        return s_next

    s_last = lax.fori_loop(0, nkb - 1, body, scores(0))
    update(nkb - 1, s_last, True)
    acc = acc_ref[...]
    o_lat = (acc[:, :A_KV_LORA] / acc[:, A_KV_LORA:]).astype(BF16)
    o = jnp.zeros((tq, HW), F32)
    for h in range(HEADS):
        o = o + jnp.dot(o_lat[h * tq:(h + 1) * tq], wuv_ref[h], preferred_element_type=F32)
    o_ref[...] = o.astype(BF16)


def _attn_a(qa, ka, wuv_pad, *, nb, s_q, l_keys, tq, tk, q_pos0, l_valid):
    nq = s_q // tq
    assert tq <= 2 * CHUNK <= tk and (tq <= CHUNK or q_pos0 % tq == 0)
    kern = functools.partial(_attn_a_kernel, tq=tq, tk=tk, q_pos0=q_pos0, l_valid=l_valid)
    return pl.pallas_call(
        kern, grid=(nb, nq),
        in_specs=[pl.BlockSpec((tq, HEADS * 256), lambda b, i: (b * nq + i, 0)),
                  pl.BlockSpec((l_keys, 256), lambda b, i: (b, 0), pipeline_mode=pl.Buffered(1)),
                  _resident(wuv_pad)],
        out_specs=pl.BlockSpec((tq, HW), lambda b, i: (b * nq + i, 0)),
        out_shape=jax.ShapeDtypeStruct((nb * s_q, HW), BF16),
        scratch_shapes=[pltpu.VMEM((HEADS * tq, 1), F32), pltpu.VMEM((HEADS * tq, 2 * LANES), F32)],
        compiler_params=_params(2), name=f"attn_a_q{s_q}")(qa, ka, wuv_pad)


def _sortable_key(x):
    bits = lax.bitcast_convert_type(x, I32)
    return bits ^ ((bits >> 31) & 0x7FFFFFFF)


def _attn_b_kernel(q_ref, iq_ref, iw_ref, k_ref, v_ref, ik_ref, o_ref, s_ref, m_ref, acc_ref,
                   *, tq, tk, q_pos0, l_valid, topk, idx_bits, group):
    qi = pl.program_id(1)
    qpos0 = q_pos0 + qi * tq
    row = lax.broadcasted_iota(I32, (tq, 1), 0)
    limit_row = jnp.minimum(((qpos0 + row) // CHUNK + 1) * CHUNK, l_valid)
    limit_max = jnp.minimum(((qpos0 + tq - 1) // CHUNK + 1) * CHUNK, l_valid)
    nkb = (limit_max + tk - 1) // tk
    kf = float(topk)

    iq = iq_ref[...]
    iqs = jnp.concatenate(
        [jnp.where(_head_half_mask((tq, LANES), h), iq[:, (h // 2) * LANES:(h // 2 + 1) * LANES],
                   jnp.zeros((tq, LANES), BF16)) for h in range(HEADS)], axis=0)
    w = iw_ref[...]

    def score_body(kb, carry):
        start = pl.multiple_of(kb * tk, tk)
        x = lax.dot_general(iqs, ik_ref[pl.ds(start, tk), :], (((1,), (1,)), ((), ())),
                            preferred_element_type=F32)
        sc = jnp.zeros((tq, tk), F32)
        for h in range(HEADS):
            sc = sc + jnp.maximum(x[h * tq:(h + 1) * tq], 0.0) * w[:, h:h + 1]
        kpos = start + lax.broadcasted_iota(I32, (1, tk), 1)
        sc = jnp.where(kpos < limit_row, sc + 0.0, NEG_INF)
        s_ref[kb] = _sortable_key(sc)
        return carry

    nkg = (nkb + group - 1) // group
    lax.fori_loop(0, nkg * group, score_body, 0)

    nsl = tk // LANES
    ones = jnp.ones((LANES, LANES), BF16)
    limit_rep = jnp.broadcast_to(limit_row, (tq, LANES))
    rh = min(tq, LANES)

    def count(pred):
        parts = []
        for r in range(tq // rh):
            rows = slice(r * rh, (r + 1) * rh)

            def body(g, c, rows=rows):
                for u in range(group):
                    kb = g * group + u
                    for j in range(nsl):
                        blk = s_ref[kb, rows, j * LANES:(j + 1) * LANES]
                        kpos = kb * tk + j * LANES + lax.broadcasted_iota(I32, (1, LANES), 1)
                        c = c + jnp.where(pred(blk, kpos, rows), 1.0, 0.0)
                return c

            parts.append(lax.fori_loop(0, nkg, body, jnp.zeros((rh, LANES), F32)))
        c = parts[0] if len(parts) == 1 else jnp.concatenate(parts, axis=0)
        return jnp.dot(c.astype(BF16), ones, preferred_element_type=F32)

    def row_max():
        parts = []
        for r in range(tq // rh):
            rows = slice(r * rh, (r + 1) * rh)

            def body(g, mx, rows=rows):
                for u in range(group):
                    for j in range(nsl):
                        mx = jnp.maximum(mx, s_ref[g * group + u, rows, j * LANES:(j + 1) * LANES])
                return mx

            parts.append(lax.fori_loop(0, nkg, body, jnp.full((rh, LANES), INT_MIN, I32)))
        mx = parts[0] if len(parts) == 1 else jnp.concatenate(parts, axis=0)
        return jnp.broadcast_to(jnp.max(mx, axis=1, keepdims=True), (tq, LANES))

    key_max = row_max()
    n_all = (nkg * group * tk).astype(F32)
    n_pos = count(lambda blk, kpos, rows: blk >= 0)
    thr0 = jnp.where(n_pos >= kf, 0, INT_MIN).astype(I32)
    n_thr0 = jnp.where(n_pos >= kf, n_pos, n_all)

    def all_rows(flag):
        return jnp.min(jnp.where(flag, 1.0, 0.0)) > 0.5

    def bit_cond(state):
        i, resolved, _, _ = state
        return (i < 31) & (resolved == 0)

    def bit_body(state):
        i, _, t, n_t = state
        cand = t + (jnp.int32(1) << (30 - i))
        c = lax.cond(all_rows(cand > key_max), lambda: jnp.zeros((tq, LANES), F32),
                     lambda: count(lambda blk, kpos, rows: blk >= cand[rows]))
        take = c >= kf
        n_t = jnp.where(take, c, n_t)
        return i + 1, all_rows(n_t == kf).astype(I32), jnp.where(take, cand, t), n_t

    _, _, thr, n_ge = lax.while_loop(
        bit_cond, bit_body, (jnp.int32(0), all_rows(n_thr0 == kf).astype(I32), thr0, n_thr0))
    excess = jnp.max(jnp.where(n_ge > kf, 1.0, 0.0)) > 0.0

    def tie_cut(_):
        need = kf - count(lambda blk, kpos, rows: blk > thr[rows])

        def jb(i, j):
            cand = j + (jnp.int32(1) << (idx_bits - 1 - i))
            c = count(lambda blk, kpos, rows: (blk == thr[rows]) & (kpos < cand[rows]))
            return jnp.where(c < need, cand, j)
        return lax.fori_loop(0, idx_bits, jb, jnp.zeros((tq, LANES), I32))

    cut = lax.cond(excess, tie_cut, lambda _: jnp.full((tq, LANES), 2 ** 30, I32), 0)

    def bias_body(kb, carry):
        for j in range(nsl):
            blk = s_ref[kb, :, j * LANES:(j + 1) * LANES]
            kpos = kb * tk + j * LANES + lax.broadcasted_iota(I32, (1, LANES), 1)
            sel = ((blk > thr) | ((blk == thr) & (kpos <= cut))) & (kpos < limit_rep)
            s_ref[kb, :, j * LANES:(j + 1) * LANES] = lax.bitcast_convert_type(
                jnp.where(sel, 0.0, NEG_INF).astype(F32), I32)
        return carry

    lax.fori_loop(0, nkb, bias_body, 0)

    m_ref[...] = jnp.full(m_ref.shape, NEG_INF, F32)
    acc_ref[...] = jnp.zeros(acc_ref.shape, F32)
    v_ones = jnp.ones((tk, LANES), BF16)

    def attn_body(kb, carry):
        start = pl.multiple_of(kb * tk, tk)
        bias = lax.bitcast_convert_type(s_ref[kb], F32)
        for pr in range(HEADS // 2):
            kblk = k_ref[pl.ds(start, tk), pr * LANES:(pr + 1) * LANES]
            vext = jnp.concatenate([v_ref[pl.ds(start, tk), pr * LANES:(pr + 1) * LANES], v_ones], axis=1)
            qpair = q_ref[:, pr * LANES:(pr + 1) * LANES]
            for h in (2 * pr, 2 * pr + 1):
                qm = jnp.where(_head_half_mask((tq, LANES), h), qpair, jnp.zeros((tq, LANES), BF16))
                s = lax.dot_general(qm, kblk, (((1,), (1,)), ((), ())), preferred_element_type=F32) + bias
                mx = s[:, 0:LANES]
                for j in range(1, nsl):
                    mx = jnp.maximum(mx, s[:, j * LANES:(j + 1) * LANES])
                m_old = m_ref[h]
                m_new = jnp.maximum(m_old, jnp.max(mx, axis=1, keepdims=True))
                m_safe = jnp.where(m_new == NEG_INF, 0.0, m_new)
                alpha = jnp.exp2(m_old - m_safe)
                p = jnp.exp2(s - m_safe).astype(BF16)
                acc_ref[h] = alpha * acc_ref[h] + jnp.dot(p, vext, preferred_element_type=F32)
                m_ref[h] = m_new
        return carry

    lax.fori_loop(0, nkb, attn_body, 0)
    lane = lax.broadcasted_iota(I32, (tq, LANES), 1)
    for pr in range(HEADS // 2):
        a_even, a_odd = acc_ref[2 * pr], acc_ref[2 * pr + 1]
        o_even = a_even[:, :LANES] / a_even[:, LANES:]
        o_odd = a_odd[:, :LANES] / a_odd[:, LANES:]
        o_ref[:, pr * LANES:(pr + 1) * LANES] = jnp.where(lane < 64, o_even, o_odd).astype(BF16)


def _attn_b(bq, iq, iw, bk, bv, ik2, *, nb, s_q, l_keys, tq, tk, q_pos0, l_valid, topk):
    nq = s_q // tq
    nkb_max = l_keys // tk
    idx_bits = max(1, int(l_keys - 1).bit_length())
    assert l_keys // LANES <= 256, "per-lane key counts must stay exact in bf16"
    group = 2 if nkb_max % 2 == 0 else 1
    kern = functools.partial(_attn_b_kernel, tq=tq, tk=tk, q_pos0=q_pos0, l_valid=l_valid,
                             topk=topk, idx_bits=idx_bits, group=group)
    qmap = lambda b, i: (b * nq + i, 0)
    kmap = lambda b, i: (b, 0)
    return pl.pallas_call(
        kern, grid=(nb, nq),
        in_specs=[pl.BlockSpec((tq, HW), qmap), pl.BlockSpec((tq, HW), qmap),
                  pl.BlockSpec((tq, LANES), qmap),
                  pl.BlockSpec((l_keys, HW), kmap, pipeline_mode=pl.Buffered(1)),
                  pl.BlockSpec((l_keys, HW), kmap, pipeline_mode=pl.Buffered(1)),
                  pl.BlockSpec((l_keys, LANES), kmap, pipeline_mode=pl.Buffered(1))],
        out_specs=pl.BlockSpec((tq, HW), qmap),
        out_shape=jax.ShapeDtypeStruct((nb * s_q, HW), BF16),
        scratch_shapes=[pltpu.VMEM((nkb_max, tq, tk), I32),
                        pltpu.VMEM((HEADS, tq, 1), F32), pltpu.VMEM((HEADS, tq, 2 * LANES), F32)],
        compiler_params=_params(2), name=f"attn_b_q{s_q}")(bq, iq, iw, bk, bv, ik2)


def _attn_c_kernel(*refs, tq, off):
    q_ref = refs[0]
    k_refs = refs[1:1 + C_BAND_BLOCKS]
    v_refs = refs[1 + C_BAND_BLOCKS:1 + 2 * C_BAND_BLOCKS]
    bias_ref = refs[1 + 2 * C_BAND_BLOCKS]
    o_ref = refs[2 + 2 * C_BAND_BLOCKS]
    kc_ref, vc_ref = refs[3 + 2 * C_BAND_BLOCKS:]
    qi = pl.program_id(1)
    for j in range(C_BAND_BLOCKS):
        kc_ref[j * C_KEY_BLOCK:(j + 1) * C_KEY_BLOCK, :] = k_refs[j][...]
        vc_ref[j * C_KEY_BLOCK:(j + 1) * C_KEY_BLOCK, :] = v_refs[j][...]
    col = lax.broadcasted_iota(I32, (1, C_BAND), 1)
    qblk = tq // C_KEY_BLOCK
    col_valid = col >= (off - qi * qblk) * C_KEY_BLOCK
    q = q_ref[...]
    o_even = None
    for h in range(HEADS):
        pr = h // 2
        qm = jnp.where(_head_half_mask((tq, LANES), h), q[:, pr * LANES:(pr + 1) * LANES],
                       jnp.zeros((tq, LANES), BF16))
        s = lax.dot_general(qm, kc_ref[:, pr * LANES:(pr + 1) * LANES], (((1,), (1,)), ((), ())),
                            preferred_element_type=F32) + bias_ref[h]
        s = jnp.where(col_valid, s, NEG_INF)
        m = jnp.max(s, axis=1, keepdims=True)
        p = jnp.exp2(s - m)
        l = jnp.sum(p, axis=1, keepdims=True)
        o_h = jnp.dot(p.astype(BF16), vc_ref[:, pr * LANES:(pr + 1) * LANES],
                      preferred_element_type=F32) / l
        if h % 2 == 0:
            o_even = o_h
        else:
            lane = lax.broadcasted_iota(I32, (tq, LANES), 1)
            o_ref[:, pr * LANES:(pr + 1) * LANES] = jnp.where(lane < 64, o_even, o_h).astype(BF16)


def _attn_c(cq, ck, cv, bias, *, nb, s_q, l_keys, tq, off):
    nq = s_q // tq
    nkblk = l_keys // C_KEY_BLOCK
    kern = functools.partial(_attn_c_kernel, tq=tq, off=off)
    qmap = lambda b, i: (b * nq + i, 0)

    qblk = tq // C_KEY_BLOCK

    def kspec(j):
        return pl.BlockSpec((C_KEY_BLOCK, HW),
                            lambda b, i: (b * nkblk + jnp.maximum(i * qblk + j - off, 0), 0))

    kv_specs = [kspec(j) for j in range(C_BAND_BLOCKS)]
    return pl.pallas_call(
        kern, grid=(nb, nq),
        in_specs=[pl.BlockSpec((tq, HW), qmap)] + kv_specs + kv_specs + [_resident(bias)],
        out_specs=pl.BlockSpec((tq, HW), qmap),
        out_shape=jax.ShapeDtypeStruct((nb * s_q, HW), BF16),
        scratch_shapes=[pltpu.VMEM((C_BAND, HW), BF16), pltpu.VMEM((C_BAND, HW), BF16)],
        compiler_params=_params(2), name=f"attn_c_q{s_q}")(
            cq, *([ck] * C_BAND_BLOCKS), *([cv] * C_BAND_BLOCKS), bias)


def _merge_kernel(x_ref, g1_ref, oa_ref, ob_ref, oc_ref, wg_ref, woa_ref, wob_ref, woc_ref, wout_ref, y_ref):
    x = x_ref[...]
    xn = _rms(x, g1_ref[...]).astype(BF16)
    merged = jnp.zeros(x.shape, F32)
    for b, (o_ref, w_ref) in enumerate(((oa_ref, woa_ref), (ob_ref, wob_ref), (oc_ref, woc_ref))):
        gate = jax.nn.sigmoid(jnp.dot(xn, wg_ref[:, b * D_MODEL:(b + 1) * D_MODEL],
                                      preferred_element_type=F32))
        merged = merged + gate * jnp.dot(o_ref[...], w_ref[...], preferred_element_type=F32)
    y_ref[...] = x + jnp.dot(merged.astype(BF16), wout_ref[...], preferred_element_type=F32)


def _mlp_kernel(x_ref, g2_ref, wup_ref, wdn_ref, gf_ref, y_ref, *, ff_chunk, final):
    x = x_ref[...]
    xn = _rms(x, g2_ref[...]).astype(BF16)
    acc = jnp.zeros(x.shape, F32)
    for c in range(D_FF // ff_chunk):
        h = jnp.dot(xn, wup_ref[:, c * ff_chunk:(c + 1) * ff_chunk], preferred_element_type=F32)
        h = jnp.square(jnp.maximum(h, 0.0)).astype(BF16)
        acc = acc + jnp.dot(h, wdn_ref[c * ff_chunk:(c + 1) * ff_chunk, :], preferred_element_type=F32)
    y = x + acc
    if final:
        y = _rms(y, gf_ref[...])
    y_ref[...] = y


def _rope_tables(pos, rot, period, live_lanes):
    half = rot // 2
    n = pos.shape[0]
    inv_freq = ROPE_THETA ** (-jnp.arange(half, dtype=F32) / half)
    ang = pos.astype(F32)[:, None] * inv_freq[None, :]
    cos, sin = jnp.cos(ang), jnp.sin(ang)
    reps = live_lanes // period
    dead = LANES - reps * period
    cos_p = jnp.concatenate([cos, cos, jnp.ones((n, period - rot), F32)], axis=1)
    sin_p = jnp.concatenate([-sin, sin, jnp.zeros((n, period - rot), F32)], axis=1)
    cos_t = jnp.concatenate([cos_p] * reps + [jnp.ones((n, dead), F32)], axis=1)
    sin_t = jnp.concatenate([sin_p] * reps + [jnp.zeros((n, dead), F32)], axis=1)
    return cos_t, sin_t


def _layer_weights(w_in, a_w_uq, a_w_uk, a_w_uv):
    offs = np.cumsum([0, A_Q_LORA, A_KV_LORA, A_ROPE, HW, HW, HW, HW, IDX_DIM, HEADS, HW, HW, HW,
                      3 * D_MODEL])
    seg = lambda i: w_in[:, offs[i]:offs[i + 1]]
    zeros = lambda n: jnp.zeros((D_MODEL, n), w_in.dtype)
    w_a = jnp.concatenate([seg(0), seg(1), seg(2), zeros(LANES - A_ROPE)], axis=1)
    w_b = jnp.concatenate([seg(3), seg(4), seg(5), seg(6), seg(7), zeros(LANES - IDX_DIM),
                           seg(8), zeros(LANES - HEADS)], axis=1)
    w_c = jnp.concatenate([seg(9), seg(10), seg(11)], axis=1)
    w_g = seg(12)
    uq = a_w_uq.reshape(A_Q_LORA, HEADS, A_NOPE + A_ROPE)
    uq_nope = uq[:, :, :A_NOPE].reshape(A_Q_LORA, HW)
    uq_rope = jnp.pad(uq[:, :, A_NOPE:], ((0, 0), (0, 0), (0, LANES - A_ROPE))).reshape(A_Q_LORA, HEADS * LANES)
    w_uq = jnp.concatenate([uq_nope, uq_rope], axis=1)
    uk = a_w_uk.reshape(A_KV_LORA, HEADS, A_NOPE)
    ukt = jnp.transpose(uk, (1, 2, 0))
    ukt_pad = jnp.stack([jnp.pad(ukt[h], ((0, 64), (0, 0)) if h % 2 == 0 else ((64, 0), (0, 0)))
                         for h in range(HEADS)], axis=0)
    uv = a_w_uv.reshape(A_KV_LORA, HEADS, A_VDIM)
    uv_pad = jnp.stack([jnp.pad(uv[:, h], ((0, 0), (h * A_VDIM, HW - (h + 1) * A_VDIM)))
                        for h in range(HEADS)], axis=0)
    bf = lambda a: a.astype(BF16)
    return bf(w_a), bf(w_b), bf(w_c), bf(w_g), bf(w_uq), bf(ukt_pad), bf(uv_pad)


def _band_bias(rel_bias, q_pos, k_pos, k_live):
    q_c, k_c = q_pos // CHUNK, k_pos // CHUNK
    vis = (k_c[None, :] <= q_c[:, None]) & (k_c[None, :] >= q_c[:, None] - C_LEFT_CHUNKS) & k_live[None, :]
    nq, nk = len(q_pos), len(k_pos)
    diag = (q_pos[0] - k_pos[0]) + (nq - 1) - np.arange(nq + nk - 1)
    ext = rel_bias[:, np.clip(diag, -C_REL_CLIP, C_REL_CLIP) + C_REL_CLIP].astype(F32) * LOG2E
    bias = jnp.stack([ext[:, nq - 1 - i:nq - 1 - i + nk] for i in range(nq)], axis=1)
    return jnp.where(vis[None], bias, NEG_INF)


def _pad_rows(a, n):
    return jnp.pad(a, ((0, 0), (0, n - a.shape[1]), (0, 0)))


def _mixers(x2d, lw, *, nb, s_q, q_pos0, past, tm, tq_a, tq_b, tq_c, tk_a, tk_b, tables, c_rel_bias):
    (g1, gq, gkv, w_a, w_b, w_c, w_uq, ukt_pad, uv_pad) = lw
    cos_a, sin_a, cos_b, sin_b, cos_k, sin_k = tables
    t = nb * s_q
    res = _resident
    qa, ka, ckv32, kr32 = _row_call(
        _prep_a_kernel, x2d, tm, (g1, w_a, gq, gkv, w_uq, ukt_pad, cos_a, sin_a),
        [res(g1), res(w_a), res(gq), res(gkv), res(w_uq), res(ukt_pad), _table_spec(cos_a, tm),
         _table_spec(sin_a, tm)],
        [(HEADS * 256, BF16), (256, BF16), (A_KV_LORA, F32), (A_ROPE, F32)], "prep_a")
    bq, bk, bv, iq, ik2, iw, bk32, bv32, ik32 = _row_call(
        _prep_b_kernel, x2d, tm, (g1, w_b, cos_b, sin_b, cos_k, sin_k),
        [res(g1), res(w_b), _table_spec(cos_b, tm), _table_spec(sin_b, tm), _table_spec(cos_k, tm),
         _table_spec(sin_k, tm)],
        [(HW, BF16), (HW, BF16), (HW, BF16), (HW, BF16), (LANES, BF16), (LANES, F32),
         (HW, F32), (HW, F32), (IDX_DIM, F32)], "prep_b")
    cq, ck, cv = _row_call(_prep_c_kernel, x2d, tm, (g1, w_c), [res(g1), res(w_c)],
                           [(HW, BF16), (HW, BF16), (HW, BF16)], "prep_c")

    if past is None:
        l_valid = s_q
        l_keys = s_q
        keys_a, keys_bk, keys_bv, keys_ik = ka, bk, bv, ik2
        keys_ck, keys_cv = ck, cv
        lc_keys = s_q
        c_off = C_REACH // C_KEY_BLOCK
        assert c_off + tq_c // C_KEY_BLOCK == C_BAND_BLOCKS
        qp = np.arange(tq_c)
        kp = np.arange(C_BAND) - C_REACH
        bias = _band_bias(c_rel_bias, qp, kp, np.ones(C_BAND, bool))
        keep = min(C_REACH, s_q)
        x_tail = x2d.reshape(nb, s_q, D_MODEL)[:, s_q - keep:].reshape(nb * keep, D_MODEL)
        w_ckv = w_c[:, HW:]
        ck32, cv32 = _row_call(_prep_c32_kernel, x_tail, min(tm, nb * keep), (g1, w_ckv),
                               [res(g1), res(w_ckv)], [(HW, F32), (HW, F32)], "prep_c32")
        ck32 = ck32.reshape(nb, keep, HEADS, C_HEAD_DIM)
        cv32 = cv32.reshape(nb, keep, HEADS, C_HEAD_DIM)
    else:
        p_ckv, p_kr, p_bk, p_bv, p_ik, p_ck, p_cv = past
        past_len = p_ckv.shape[1]
        l_valid = past_len + s_q
        l_keys = -(-l_valid // LANES) * LANES
        r3 = lambda a: a.reshape(nb, s_q, a.shape[-1])
        cat = lambda old, new: _pad_rows(jnp.concatenate([old.astype(BF16), r3(new)], axis=1), l_keys)
        flat = lambda a: a.reshape(nb * a.shape[1], a.shape[2])
        old_a = jnp.concatenate([p_ckv, p_kr, jnp.zeros((nb, past_len, 256 - A_KV_LORA - A_ROPE), F32)], axis=-1)
        keys_a = flat(cat(old_a, ka))
        keys_bk = flat(cat(p_bk.reshape(nb, past_len, HW), bk))
        keys_bv = flat(cat(p_bv.reshape(nb, past_len, HW), bv))
        keys_ik = flat(cat(jnp.concatenate([p_ik, p_ik], axis=-1), ik2))
        w_c_len = p_ck.shape[1]
        lc_valid = w_c_len + s_q
        lc_keys = C_BAND
        catc = lambda old, new: _pad_rows(jnp.concatenate([old.astype(BF16), r3(new)], axis=1), lc_keys)
        keys_ck = flat(catc(p_ck.reshape(nb, w_c_len, HW), ck))
        keys_cv = flat(catc(p_cv.reshape(nb, w_c_len, HW), cv))
        c_off = 0
        qp = np.arange(q_pos0, q_pos0 + s_q)
        kp = np.arange(past_len - w_c_len, past_len - w_c_len + lc_keys)
        bias = _band_bias(c_rel_bias, qp, kp, np.arange(lc_keys) < lc_valid)
        ck32, cv32 = _row_call(_prep_c32_kernel, x2d, tm, (g1, w_c[:, HW:]),
                               [res(g1), res(w_c[:, HW:])], [(HW, F32), (HW, F32)], "prep_c32")
        ck32 = ck32.reshape(nb, s_q, HEADS, C_HEAD_DIM)
        cv32 = cv32.reshape(nb, s_q, HEADS, C_HEAD_DIM)

    topk = min(B_TOPK_MAX, l_valid // 4)
    tk_a = min(tk_a, l_keys)
    tk_b = min(tk_b, l_keys)
    o_a = _attn_a(qa, keys_a, uv_pad, nb=nb, s_q=s_q, l_keys=l_keys, tq=tq_a, tk=tk_a,
                  q_pos0=q_pos0, l_valid=l_valid)
    o_b = _attn_b(bq, iq, iw, keys_bk, keys_bv, keys_ik, nb=nb, s_q=s_q, l_keys=l_keys, tq=tq_b,
                  tk=tk_b, q_pos0=q_pos0, l_valid=l_valid, topk=topk)
    o_c = _attn_c(cq, keys_ck, keys_cv, bias, nb=nb, s_q=s_q, l_keys=lc_keys, tq=tq_c, off=c_off)
    new_rows = (ckv32.reshape(nb, s_q, A_KV_LORA), kr32.reshape(nb, s_q, A_ROPE),
                bk32.reshape(nb, s_q, HEADS, B_HEAD_DIM), bv32.reshape(nb, s_q, HEADS, B_HEAD_DIM),
                ik32.reshape(nb, s_q, IDX_DIM), ck32, cv32)
    return o_a, o_b, o_c, new_rows


def _layer(x2d, lw_mix, lw_rest, gf, *, final, tm, **kw):
    o_a, o_b, o_c, new_rows = _mixers(x2d, lw_mix, tm=tm, **kw)
    g1 = lw_mix[0]
    (w_g, w_oa, w_ob, w_oc, w_out, g2, w_up, w_dn) = lw_rest
    res = _resident
    row = lambda w: pl.BlockSpec((tm, w), lambda i: (i, 0))
    (x1,) = _row_call(_merge_kernel, x2d, tm, (g1, o_a, o_b, o_c, w_g, w_oa, w_ob, w_oc, w_out),
                      [res(g1), row(HW), row(HW), row(HW), res(w_g), res(w_oa), res(w_ob), res(w_oc),
                       res(w_out)], [(D_MODEL, F32)], "merge")
    (x2,) = _row_call(functools.partial(_mlp_kernel, ff_chunk=1024, final=final), x1, tm,
                      (g2, w_up, w_dn, gf), [res(g2), res(w_up), res(w_dn), res(gf)], [(D_MODEL, F32)],
                      "mlp_final" if final else "mlp")
    return x2, new_rows


def _tiles(nb, s_q, prompt):
    if prompt:
        tm = 512 if (nb * s_q) % 512 == 0 else 256
        return dict(tm=tm, tq_a=128, tq_b=256, tq_c=256, tk_a=512, tk_b=512)
    whole = 1 << 30
    return dict(tm=nb * s_q, tq_a=s_q, tq_b=s_q, tq_c=s_q, tk_a=whole, tk_b=whole)


def kernel(x_prompt, x_sample, cache_a_ckv, cache_a_krope, cache_b_k, cache_b_v, cache_b_idx_k, cache_c_k, cache_c_v, norm1, w_in, a_q_norm, a_kv_norm, a_w_uq, a_w_uk, a_w_uv, c_rel_bias, w_oa, w_ob, w_oc, w_out, norm2, w_up, w_down, final_norm):
    depth = w_in.shape[0]
    nb_p, s_p, _ = x_prompt.shape
    nb_s, s_s, _ = x_sample.shape
    past_len = cache_a_ckv.shape[2]
    assert s_p % 256 == 0 and CHUNK % s_s == 0 and past_len % CHUNK == 0
    tiles_p, tiles_s = _tiles(nb_p, s_p, prompt=True), _tiles(nb_s, s_s, prompt=False)

    pos_p = jnp.arange(s_p, dtype=jnp.int32)
    pos_s = jnp.tile(jnp.arange(past_len, past_len + s_s, dtype=jnp.int32), nb_s)

    def tables(pos):
        return (_rope_tables(pos, A_ROPE, A_ROPE, A_ROPE) + _rope_tables(pos, B_ROT, B_HEAD_DIM, LANES)
                + _rope_tables(pos, IDX_ROT, IDX_DIM, IDX_DIM))

    tab_p, tab_s = tables(pos_p), tables(pos_s)
    gf = final_norm.reshape(1, D_MODEL)
    bf = lambda a: a.astype(BF16)

    xp = x_prompt.reshape(nb_p * s_p, D_MODEL)
    xs = x_sample.reshape(nb_s * s_s, D_MODEL)
    rows_p, rows_s = [], []
    for l in range(depth):
        w_a, w_b, w_c, w_g, w_uq, ukt_pad, uv_pad = _layer_weights(w_in[l], a_w_uq[l], a_w_uk[l], a_w_uv[l])
        lw_mix = (norm1[l].reshape(1, -1), a_q_norm[l].reshape(1, -1), a_kv_norm[l].reshape(1, -1),
                  w_a, w_b, w_c, w_uq, ukt_pad, uv_pad)
        lw_rest = (w_g, bf(w_oa[l]), bf(w_ob[l]), bf(w_oc[l]), bf(w_out[l]), norm2[l].reshape(1, -1),
                   bf(w_up[l]), bf(w_down[l]))
        final = l == depth - 1
        past = (cache_a_ckv[l], cache_a_krope[l], cache_b_k[l], cache_b_v[l], cache_b_idx_k[l],
                cache_c_k[l], cache_c_v[l])
        xp, new_p = _layer(xp, lw_mix, lw_rest, gf, final=final, nb=nb_p, s_q=s_p, q_pos0=0,
                           past=None, tables=tab_p, c_rel_bias=c_rel_bias[l], **tiles_p)
        xs, new_s = _layer(xs, lw_mix, lw_rest, gf, final=final, nb=nb_s, s_q=s_s, q_pos0=past_len,
                           past=past, tables=tab_s, c_rel_bias=c_rel_bias[l], **tiles_s)
        rows_p.append(new_p)
        rows_s.append(new_s)

    y_prompt = xp.reshape(nb_p, s_p, D_MODEL)
    y_sample = xs.reshape(nb_s, s_s, D_MODEL)
    stack = lambda rows, i: jnp.stack([r[i] for r in rows], axis=0)
    outs = [y_prompt, y_sample]
    for i in range(7):
        outs += [stack(rows_p, i), stack(rows_s, i)]
    return tuple(outs)
```

```python
import functools

import numpy as np
import jax
import jax.numpy as jnp
from jax import lax
from jax.experimental import pallas as pl
from jax.experimental.pallas import tpu as pltpu

F32 = jnp.float32
BF16 = jnp.bfloat16
I32 = jnp.int32

D_MODEL = 1024
CHUNK = 64
ROPE_THETA = 500000.0
RMS_EPS = 1e-6
HEADS = 8
A_Q_LORA = 256
A_KV_LORA = 128
A_NOPE = 64
A_ROPE = 32
A_VDIM = 64
A_SCALE = (A_NOPE + A_ROPE) ** -0.5
B_HEAD_DIM = 64
B_ROT = B_HEAD_DIM // 4
B_SCALE = B_HEAD_DIM ** -0.5
B_TOPK_MAX = 256
IDX_DIM = 64
IDX_ROT = IDX_DIM // 4
C_HEAD_DIM = 64
C_SCALE = C_HEAD_DIM ** -0.5
C_LEFT_CHUNKS = 8
C_REACH = C_LEFT_CHUNKS * CHUNK
C_REL_CLIP = 128
D_FF = 4 * D_MODEL
HW = HEADS * 64
LANES = 128
C_KEY_BLOCK = 128
C_BAND_BLOCKS = 6
C_BAND = C_KEY_BLOCK * C_BAND_BLOCKS
VMEM_LIMIT = 56 * 1024 * 1024
NEG_INF = float("-inf")
LOG2E = 1.4426950408889634
INT_MIN = -2 ** 31


def _rms(x, g):
    ms = jnp.mean(x * x, axis=-1, keepdims=True)
    return x * lax.rsqrt(ms + RMS_EPS) * g


def _rope_lanes(x, cos, sin, half):
    w = x.shape[-1]
    outs = []
    for j in range(w // LANES):
        xs = x[:, j * LANES:(j + 1) * LANES]
        lane = lax.broadcasted_iota(I32, xs.shape, 1)
        first = (lane % (2 * half)) < half
        partner = jnp.where(first, pltpu.roll(xs, LANES - half, 1), pltpu.roll(xs, half, 1))
        outs.append(xs * cos + partner * sin)
    return outs[0] if len(outs) == 1 else jnp.concatenate(outs, axis=1)


def _head_half_mask(shape, head):
    lane = lax.broadcasted_iota(I32, shape, 1)
    return (lane < 64) if head % 2 == 0 else (lane >= 64)


def _resident(arr):
    nd = arr.ndim
    return pl.BlockSpec(arr.shape, lambda *_: (0,) * nd, pipeline_mode=pl.Buffered(1))


def _params(n_axes):
    return pltpu.CompilerParams(dimension_semantics=("arbitrary",) * n_axes,
                                vmem_limit_bytes=VMEM_LIMIT)


def _prep_a_kernel(x_ref, g1_ref, w_ref, gq_ref, gkv_ref, wuq_ref, wukt_ref, cos_ref, sin_ref,
                   qa_ref, ka_ref, ckv_ref, kr_ref):
    xn = _rms(x_ref[...], g1_ref[...]).astype(BF16)
    z = jnp.dot(xn, w_ref[...], preferred_element_type=F32)
    aq = z[:, :A_Q_LORA]
    akv = z[:, A_Q_LORA:A_Q_LORA + A_KV_LORA]
    akr = z[:, A_Q_LORA + A_KV_LORA:]
    cos = cos_ref[...]
    sin = sin_ref[...]
    ckv = _rms(akv, gkv_ref[...])
    ckv_ref[...] = ckv
    kr = _rope_lanes(akr, cos, sin, A_ROPE // 2)
    kr_ref[...] = kr[:, :A_ROPE]
    ka_ref[...] = jnp.concatenate([ckv, kr], axis=1).astype(BF16)
    aqn = _rms(aq, gq_ref[...]).astype(BF16)
    qa = jnp.dot(aqn, wuq_ref[...], preferred_element_type=F32)
    for h in range(HEADS):
        pair = qa[:, (h // 2) * LANES:(h // 2 + 1) * LANES].astype(BF16)
        q_lat = jnp.dot(pair, wukt_ref[h], preferred_element_type=F32)
        q_rope = _rope_lanes(qa[:, HW + h * LANES:HW + (h + 1) * LANES], cos, sin, A_ROPE // 2)
        qa_ref[:, h * 256:(h + 1) * 256] = (
            jnp.concatenate([q_lat, q_rope], axis=1) * (A_SCALE * LOG2E)).astype(BF16)


def _prep_b_kernel(x_ref, g1_ref, w_ref, cos_ref, sin_ref, cosk_ref, sink_ref,
                   bq_ref, bk_ref, bv_ref, iq_ref, ik2_ref, iw_ref, bk32_ref, bv32_ref, ik32_ref):
    xn = _rms(x_ref[...], g1_ref[...]).astype(BF16)
    z = jnp.dot(xn, w_ref[...], preferred_element_type=F32)
    cos = cos_ref[...]
    sin = sin_ref[...]
    bq = _rope_lanes(z[:, 0:HW], cos, sin, B_ROT // 2)
    bq_ref[...] = (bq * (B_SCALE * LOG2E)).astype(BF16)
    bk = _rope_lanes(z[:, HW:2 * HW], cos, sin, B_ROT // 2)
    bk32_ref[...] = bk
    bk_ref[...] = bk.astype(BF16)
    bv = z[:, 2 * HW:3 * HW]
    bv32_ref[...] = bv
    bv_ref[...] = bv.astype(BF16)
    iq = _rope_lanes(z[:, 3 * HW:4 * HW], cos, sin, IDX_ROT // 2)
    iq_ref[...] = (iq * IDX_DIM ** -0.5).astype(BF16)
    ik = _rope_lanes(z[:, 4 * HW:4 * HW + LANES], cosk_ref[...], sink_ref[...], IDX_ROT // 2)
    ik32_ref[...] = ik[:, :IDX_DIM]
    lane = lax.broadcasted_iota(I32, ik.shape, 1)
    ik2_ref[...] = jnp.where(lane < IDX_DIM, ik, pltpu.roll(ik, IDX_DIM, 1)).astype(BF16)
    iw_ref[...] = z[:, 4 * HW + LANES:] * HEADS ** -0.5


def _prep_c_kernel(x_ref, g1_ref, w_ref, cq_ref, ck_ref, cv_ref):
    xn = _rms(x_ref[...], g1_ref[...]).astype(BF16)
    z = jnp.dot(xn, w_ref[...], preferred_element_type=F32)
    cq_ref[...] = (z[:, 0:HW] * (C_SCALE * LOG2E)).astype(BF16)
    ck_ref[...] = z[:, HW:2 * HW].astype(BF16)
    cv_ref[...] = z[:, 2 * HW:].astype(BF16)


def _prep_c32_kernel(x_ref, g1_ref, w_ref, ck_ref, cv_ref):
    xn = _rms(x_ref[...], g1_ref[...]).astype(BF16)
    z = jnp.dot(xn, w_ref[...], preferred_element_type=F32)
    ck_ref[...] = z[:, 0:HW]
    cv_ref[...] = z[:, HW:]


def _row_call(kernel, x, tm, operands, specs, outs, name):
    t = x.shape[0]
    name = f"{name}_t{t}"
    grid = (t // tm,)
    in_specs = [pl.BlockSpec((tm, x.shape[1]), lambda i: (i, 0))] + specs
    out_shape = [jax.ShapeDtypeStruct((t, w), dt) for w, dt in outs]
    out_specs = [pl.BlockSpec((tm, w), lambda i: (i, 0)) for w, _ in outs]
    return pl.pallas_call(kernel, grid=grid, in_specs=in_specs, out_specs=out_specs,
                          out_shape=out_shape, compiler_params=_params(1), name=name)(x, *operands)


def _table_spec(table, tm):
    nblk = table.shape[0] // tm
    return pl.BlockSpec((tm, table.shape[1]), lambda i: (i % nblk, 0))


def _attn_a_kernel(q_ref, k_ref, wuv_ref, o_ref, m_ref, acc_ref, *, tq, tk, q_pos0, l_valid):
    qi = pl.program_id(1)
    qs = jnp.concatenate([q_ref[:, h * 256:(h + 1) * 256] for h in range(HEADS)], axis=0)
    qpos0 = q_pos0 + qi * tq
    qrow = lax.rem(lax.broadcasted_iota(I32, (HEADS * tq, 1), 0), tq)
    limit_row = jnp.minimum(((qpos0 + qrow) // CHUNK + 1) * CHUNK, l_valid)
    limit_max = jnp.minimum(((qpos0 + tq - 1) // CHUNK + 1) * CHUNK, l_valid)
    nkb = (limit_max + tk - 1) // tk
    m_ref[...] = jnp.full(m_ref.shape, NEG_INF, F32)
    acc_ref[...] = jnp.zeros(acc_ref.shape, F32)
    v_ones = jnp.ones((tk, LANES), BF16)

    def scores(kb):
        start = pl.multiple_of(kb * tk, tk)
        return lax.dot_general(qs, k_ref[pl.ds(start, tk), :], (((1,), (1,)), ((), ())),
                               preferred_element_type=F32)

    def update(kb, s, masked):
        start = pl.multiple_of(kb * tk, tk)
        if masked:
            kpos = start + lax.broadcasted_iota(I32, (1, tk), 1)
            s = jnp.where(kpos < limit_row, s, NEG_INF)
        mx = s[:, 0:LANES]
        for j in range(1, tk // LANES):
            mx = jnp.maximum(mx, s[:, j * LANES:(j + 1) * LANES])
        m_old = m_ref[...]
        m_new = jnp.maximum(m_old, jnp.max(mx, axis=1, keepdims=True))
        alpha = jnp.exp2(m_old - m_new)
        p = jnp.exp2(s - m_new).astype(BF16)
        vext = jnp.concatenate([k_ref[pl.ds(start, tk), :A_KV_LORA], v_ones], axis=1)
        acc_ref[...] = alpha * acc_ref[...] + jnp.dot(p, vext, preferred_element_type=F32)
        m_ref[...] = m_new

    def body(kb, s):
        s_next = scores(kb + 1)
        update(kb, s, False)
        return s_next

    s_last = lax.fori_loop(0, nkb - 1, body, scores(0))
    update(nkb - 1, s_last, True)
    acc = acc_ref[...]
    o_lat = (acc[:, :A_KV_LORA] / acc[:, A_KV_LORA:]).astype(BF16)
    o = jnp.zeros((tq, HW), F32)
    for h in range(HEADS):
        o = o + jnp.dot(o_lat[h * tq:(h + 1) * tq], wuv_ref[h], preferred_element_type=F32)
    o_ref[...] = o.astype(BF16)


def _attn_a(qa, ka, wuv_pad, *, nb, s_q, l_keys, tq, tk, q_pos0, l_valid):
    nq = s_q // tq
    assert tq <= 2 * CHUNK <= tk and (tq <= CHUNK or q_pos0 % tq == 0)
    kern = functools.partial(_attn_a_kernel, tq=tq, tk=tk, q_pos0=q_pos0, l_valid=l_valid)
    return pl.pallas_call(
        kern, grid=(nb, nq),
        in_specs=[pl.BlockSpec((tq, HEADS * 256), lambda b, i: (b * nq + i, 0)),
                  pl.BlockSpec((l_keys, 256), lambda b, i: (b, 0), pipeline_mode=pl.Buffered(1)),
                  _resident(wuv_pad)],
        out_specs=pl.BlockSpec((tq, HW), lambda b, i: (b * nq + i, 0)),
        out_shape=jax.ShapeDtypeStruct((nb * s_q, HW), BF16),
        scratch_shapes=[pltpu.VMEM((HEADS * tq, 1), F32), pltpu.VMEM((HEADS * tq, 2 * LANES), F32)],
        compiler_params=_params(2), name=f"attn_a_q{s_q}")(qa, ka, wuv_pad)


def _sortable_key(x):
    bits = lax.bitcast_convert_type(x, I32)
    return bits ^ ((bits >> 31) & 0x7FFFFFFF)


def _attn_b_kernel(q_ref, iq_ref, iw_ref, k_ref, v_ref, ik_ref, o_ref, s_ref, bias_ref, m_ref, acc_ref,
                   *, tq, tk, q_pos0, l_valid, topk, idx_bits, group):
    qi = pl.program_id(1)
    qpos0 = q_pos0 + qi * tq
    row = lax.broadcasted_iota(I32, (tq, 1), 0)
    limit_row = jnp.minimum(((qpos0 + row) // CHUNK + 1) * CHUNK, l_valid)
    limit_max = jnp.minimum(((qpos0 + tq - 1) // CHUNK + 1) * CHUNK, l_valid)
    nkb = (limit_max + tk - 1) // tk
    kf = float(topk)

    iq = iq_ref[...]
    iqs = jnp.concatenate(
        [jnp.where(_head_half_mask((tq, LANES), h), iq[:, (h // 2) * LANES:(h // 2 + 1) * LANES],
                   jnp.zeros((tq, LANES), BF16)) for h in range(HEADS)], axis=0)
    w = iw_ref[...]

    def score_body(kb, carry):
        start = pl.multiple_of(kb * tk, tk)
        x = lax.dot_general(iqs, ik_ref[pl.ds(start, tk), :], (((1,), (1,)), ((), ())),
                            preferred_element_type=F32)
        sc = jnp.zeros((tq, tk), F32)
        for h in range(HEADS):
            sc = sc + jnp.maximum(x[h * tq:(h + 1) * tq], 0.0) * w[:, h:h + 1]
        kpos = start + lax.broadcasted_iota(I32, (1, tk), 1)
        sc = jnp.where(kpos < limit_row, sc + 0.0, NEG_INF)
        s_ref[kb] = _sortable_key(sc)
        return carry

    nkg = (nkb + group - 1) // group
    nsl = tk // LANES
    if tq % LANES == 0:
        _select_keys_on_sublanes(iqs, iw_ref, ik_ref, s_ref, bias_ref, tq=tq, tk=tk, qpos0=qpos0,
                                 l_valid=l_valid, kf=kf, idx_bits=idx_bits, group=group, nkb=nkb, nkg=nkg)
    else:
        lax.fori_loop(0, nkg * group, score_body, 0)
        _select_keys_on_lanes(s_ref, bias_ref, limit_row, tq=tq, tk=tk, kf=kf, idx_bits=idx_bits,
                              group=group, nkb=nkb, nkg=nkg)
    _masked_attention(q_ref, k_ref, v_ref, bias_ref, o_ref, m_ref, acc_ref, tq=tq, tk=tk, nkb=nkb)


def _select_keys_on_lanes(s_ref, bias_ref, limit_row, *, tq, tk, kf, idx_bits, group, nkb, nkg):
    nsl = tk // LANES
    ones = jnp.ones((LANES, LANES), BF16)
    limit_rep = jnp.broadcast_to(limit_row, (tq, LANES))
    rh = min(tq, LANES)

    def count(pred):
        parts = []
        for r in range(tq // rh):
            rows = slice(r * rh, (r + 1) * rh)

            def body(g, c, rows=rows):
                for u in range(group):
                    kb = g * group + u
                    for j in range(nsl):
                        blk = s_ref[kb, rows, j * LANES:(j + 1) * LANES]
                        kpos = kb * tk + j * LANES + lax.broadcasted_iota(I32, (1, LANES), 1)
                        c = c + jnp.where(pred(blk, kpos, rows), 1.0, 0.0)
                return c

            parts.append(lax.fori_loop(0, nkg, body, jnp.zeros((rh, LANES), F32)))
        c = parts[0] if len(parts) == 1 else jnp.concatenate(parts, axis=0)
        return jnp.dot(c.astype(BF16), ones, preferred_element_type=F32)

    def row_max():
        parts = []
        for r in range(tq // rh):
            rows = slice(r * rh, (r + 1) * rh)

            def body(g, mx, rows=rows):
                for u in range(group):
                    for j in range(nsl):
                        mx = jnp.maximum(mx, s_ref[g * group + u, rows, j * LANES:(j + 1) * LANES])
                return mx

            parts.append(lax.fori_loop(0, nkg, body, jnp.full((rh, LANES), INT_MIN, I32)))
        mx = parts[0] if len(parts) == 1 else jnp.concatenate(parts, axis=0)
        return jnp.broadcast_to(jnp.max(mx, axis=1, keepdims=True), (tq, LANES))

    key_max = row_max()
    n_all = (nkg * group * tk).astype(F32)
    n_pos = count(lambda blk, kpos, rows: blk >= 0)
    thr0 = jnp.where(n_pos >= kf, 0, INT_MIN).astype(I32)
    n_thr0 = jnp.where(n_pos >= kf, n_pos, n_all)

    def all_rows(flag):
        return jnp.min(jnp.where(flag, 1.0, 0.0)) > 0.5

    def bit_cond(state):
        i, resolved, _, _ = state
        return (i < 31) & (resolved == 0)

    def bit_body(state):
        i, _, t, n_t = state
        cand = t + (jnp.int32(1) << (30 - i))
        c = lax.cond(all_rows(cand > key_max), lambda: jnp.zeros((tq, LANES), F32),
                     lambda: count(lambda blk, kpos, rows: blk >= cand[rows]))
        take = c >= kf
        n_t = jnp.where(take, c, n_t)
        return i + 1, all_rows(n_t == kf).astype(I32), jnp.where(take, cand, t), n_t

    _, _, thr, n_ge = lax.while_loop(
        bit_cond, bit_body, (jnp.int32(0), all_rows(n_thr0 == kf).astype(I32), thr0, n_thr0))
    excess = jnp.max(jnp.where(n_ge > kf, 1.0, 0.0)) > 0.0

    def tie_cut(_):
        need = kf - count(lambda blk, kpos, rows: blk > thr[rows])

        def jb(i, j):
            cand = j + (jnp.int32(1) << (idx_bits - 1 - i))
            c = count(lambda blk, kpos, rows: (blk == thr[rows]) & (kpos < cand[rows]))
            return jnp.where(c < need, cand, j)
        return lax.fori_loop(0, idx_bits, jb, jnp.zeros((tq, LANES), I32))

    cut = lax.cond(excess, tie_cut, lambda _: jnp.full((tq, LANES), 2 ** 30, I32), 0)

    def bias_body(kb, carry):
        for j in range(nsl):
            blk = s_ref[kb, :, j * LANES:(j + 1) * LANES]
            kpos = kb * tk + j * LANES + lax.broadcasted_iota(I32, (1, LANES), 1)
            sel = ((blk > thr) | ((blk == thr) & (kpos <= cut))) & (kpos < limit_rep)
            bias_ref[kb, :, j * LANES:(j + 1) * LANES] = jnp.where(sel, 0.0, NEG_INF).astype(F32)
        return carry

    lax.fori_loop(0, nkb, bias_body, 0)


def _masked_attention(q_ref, k_ref, v_ref, bias_ref, o_ref, m_ref, acc_ref, *, tq, tk, nkb):
    nsl = tk // LANES
    m_ref[...] = jnp.full(m_ref.shape, NEG_INF, F32)
    acc_ref[...] = jnp.zeros(acc_ref.shape, F32)
    v_ones = jnp.ones((tk, LANES), BF16)

    def attn_body(kb, carry):
        start = pl.multiple_of(kb * tk, tk)
        bias = bias_ref[kb]
        for pr in range(HEADS // 2):
            kblk = k_ref[pl.ds(start, tk), pr * LANES:(pr + 1) * LANES]
            vext = jnp.concatenate([v_ref[pl.ds(start, tk), pr * LANES:(pr + 1) * LANES], v_ones], axis=1)
            qpair = q_ref[:, pr * LANES:(pr + 1) * LANES]
            for h in (2 * pr, 2 * pr + 1):
                qm = jnp.where(_head_half_mask((tq, LANES), h), qpair, jnp.zeros((tq, LANES), BF16))
                s = lax.dot_general(qm, kblk, (((1,), (1,)), ((), ())), preferred_element_type=F32) + bias
                mx = s[:, 0:LANES]
                for j in range(1, nsl):
                    mx = jnp.maximum(mx, s[:, j * LANES:(j + 1) * LANES])
                m_old = m_ref[h]
                m_new = jnp.maximum(m_old, jnp.max(mx, axis=1, keepdims=True))
                m_safe = jnp.where(m_new == NEG_INF, 0.0, m_new)
                alpha = jnp.exp2(m_old - m_safe)
                p = jnp.exp2(s - m_safe).astype(BF16)
                acc_ref[h] = alpha * acc_ref[h] + jnp.dot(p, vext, preferred_element_type=F32)
                m_ref[h] = m_new
        return carry

    lax.fori_loop(0, nkb, attn_body, 0)
    lane = lax.broadcasted_iota(I32, (tq, LANES), 1)
    for pr in range(HEADS // 2):
        a_even, a_odd = acc_ref[2 * pr], acc_ref[2 * pr + 1]
        o_even = a_even[:, :LANES] / a_even[:, LANES:]
        o_odd = a_odd[:, :LANES] / a_odd[:, LANES:]
        o_ref[:, pr * LANES:(pr + 1) * LANES] = jnp.where(lane < 64, o_even, o_odd).astype(BF16)


def _select_keys_on_sublanes(iqs, iw_ref, ik_ref, st_ref, bias_ref, *, tq, tk, qpos0, l_valid, kf, idx_bits,
                             group, nkb, nkg):
    sub = 8
    nacc = 4
    lane_q = lax.broadcasted_iota(I32, (1, tq), 1)
    limit_t = jnp.minimum(((qpos0 + lane_q) // CHUNK + 1) * CHUNK, l_valid)
    w_t = jnp.transpose(iw_ref[...])

    def score_body(kb, carry):
        start = pl.multiple_of(kb * tk, tk)
        xt = lax.dot_general(ik_ref[pl.ds(start, tk), :], iqs, (((1,), (1,)), ((), ())),
                             preferred_element_type=F32)
        sc = jnp.zeros((tk, tq), F32)
        for h in range(HEADS):
            sc = sc + jnp.maximum(xt[:, h * tq:(h + 1) * tq], 0.0) * w_t[h:h + 1, :]
        kpos = start + lax.broadcasted_iota(I32, (tk, 1), 0)
        sc = jnp.where(kpos < limit_t, sc + 0.0, NEG_INF)
        st_ref[kb] = _sortable_key(sc)
        return carry

    lax.fori_loop(0, nkg * group, score_body, 0)

    def sweep(init, step, finish):
        def body(g, accs):
            accs = list(accs)
            for u in range(group):
                kb = g * group + u
                for j in range(tk // sub):
                    blk = st_ref[kb, j * sub:(j + 1) * sub, :]
                    kpos = kb * tk + j * sub + lax.broadcasted_iota(I32, (sub, 1), 0)
                    accs[j % nacc] = step(accs[j % nacc], blk, kpos)
            return tuple(accs)
        return finish(lax.fori_loop(0, nkg, body, tuple(init for _ in range(nacc))))

    def rep(x):
        return jnp.broadcast_to(x, (sub, tq))

    def count(pred):
        return sweep(jnp.zeros((sub, tq), F32),
                     lambda acc, blk, kpos: acc + jnp.where(pred(blk, kpos), 1.0, 0.0),
                     lambda accs: rep(jnp.sum(sum(accs[1:], accs[0]), axis=0, keepdims=True)))

    n_all = (nkg * group * tk).astype(F32)
    n_pos = count(lambda blk, kpos: blk >= 0)
    thr0 = jnp.where(n_pos >= kf, 0, INT_MIN).astype(I32)
    n_thr0 = jnp.where(n_pos >= kf, n_pos, n_all)

    def bit_body(i, state):
        t, n_t = state
        cand = t + (jnp.int32(1) << (30 - i))
        c = count(lambda blk, kpos: blk >= cand)
        take = c >= kf
        return jnp.where(take, cand, t), jnp.where(take, c, n_t)

    thr, n_ge = lax.fori_loop(0, 31, bit_body, (thr0, n_thr0))
    excess = jnp.max(jnp.where(n_ge > kf, 1.0, 0.0)) > 0.0

    def tie_cut(_):
        need = kf - count(lambda blk, kpos: blk > thr)

        def jb(i, j):
            cand = j + (jnp.int32(1) << (idx_bits - 1 - i))
            c = count(lambda blk, kpos: (blk == thr) & (kpos < cand))
            return jnp.where(c < need, cand, j)
        return lax.fori_loop(0, idx_bits, jb, jnp.zeros((sub, tq), I32))

    cut = lax.cond(excess, tie_cut, lambda _: jnp.full((sub, tq), 2 ** 30, I32), 0)
    thr_q, cut_q = thr[0:1, :], cut[0:1, :]

    def bias_body(kb, carry):
        blk = st_ref[kb]
        kpos = kb * tk + lax.broadcasted_iota(I32, (tk, 1), 0)
        sel = ((blk > thr_q) | ((blk == thr_q) & (kpos <= cut_q))) & (kpos < limit_t)
        bias_ref[kb] = jnp.transpose(jnp.where(sel, 0.0, NEG_INF).astype(F32))
        return carry

    lax.fori_loop(0, nkb, bias_body, 0)


def _attn_b(bq, iq, iw, bk, bv, ik2, *, nb, s_q, l_keys, tq, tk, q_pos0, l_valid, topk):
    nq = s_q // tq
    nkb_max = l_keys // tk
    idx_bits = max(1, int(l_keys - 1).bit_length())
    assert l_keys // LANES <= 256, "per-lane key counts must stay exact in bf16"
    group = 2 if nkb_max % 2 == 0 else 1
    kern = functools.partial(_attn_b_kernel, tq=tq, tk=tk, q_pos0=q_pos0, l_valid=l_valid,
                             topk=topk, idx_bits=idx_bits, group=group)
    qmap = lambda b, i: (b * nq + i, 0)
    kmap = lambda b, i: (b, 0)
    return pl.pallas_call(
        kern, grid=(nb, nq),
        in_specs=[pl.BlockSpec((tq, HW), qmap), pl.BlockSpec((tq, HW), qmap),
                  pl.BlockSpec((tq, LANES), qmap),
                  pl.BlockSpec((l_keys, HW), kmap, pipeline_mode=pl.Buffered(1)),
                  pl.BlockSpec((l_keys, HW), kmap, pipeline_mode=pl.Buffered(1)),
                  pl.BlockSpec((l_keys, LANES), kmap, pipeline_mode=pl.Buffered(1))],
        out_specs=pl.BlockSpec((tq, HW), qmap),
        out_shape=jax.ShapeDtypeStruct((nb * s_q, HW), BF16),
        scratch_shapes=[pltpu.VMEM((nkb_max, tk, tq) if tq % LANES == 0 else (nkb_max, tq, tk), I32),
                        pltpu.VMEM((nkb_max, tq, tk), F32),
                        pltpu.VMEM((HEADS, tq, 1), F32), pltpu.VMEM((HEADS, tq, 2 * LANES), F32)],
        compiler_params=_params(2), name=f"attn_b_q{s_q}")(bq, iq, iw, bk, bv, ik2)


def _attn_c_kernel(*refs, tq, off):
    q_ref = refs[0]
    k_refs = refs[1:1 + C_BAND_BLOCKS]
    v_refs = refs[1 + C_BAND_BLOCKS:1 + 2 * C_BAND_BLOCKS]
    bias_ref = refs[1 + 2 * C_BAND_BLOCKS]
    o_ref = refs[2 + 2 * C_BAND_BLOCKS]
    kc_ref, vc_ref = refs[3 + 2 * C_BAND_BLOCKS:]
    qi = pl.program_id(1)
    for j in range(C_BAND_BLOCKS):
        kc_ref[j * C_KEY_BLOCK:(j + 1) * C_KEY_BLOCK, :] = k_refs[j][...]
        vc_ref[j * C_KEY_BLOCK:(j + 1) * C_KEY_BLOCK, :] = v_refs[j][...]
    col = lax.broadcasted_iota(I32, (1, C_BAND), 1)
    qblk = tq // C_KEY_BLOCK
    col_valid = col >= (off - qi * qblk) * C_KEY_BLOCK
    q = q_ref[...]
    o_even = None
    for h in range(HEADS):
        pr = h // 2
        qm = jnp.where(_head_half_mask((tq, LANES), h), q[:, pr * LANES:(pr + 1) * LANES],
                       jnp.zeros((tq, LANES), BF16))
        s = lax.dot_general(qm, kc_ref[:, pr * LANES:(pr + 1) * LANES], (((1,), (1,)), ((), ())),
                            preferred_element_type=F32) + bias_ref[h]
        s = jnp.where(col_valid, s, NEG_INF)
        m = jnp.max(s, axis=1, keepdims=True)
        p = jnp.exp2(s - m)
        l = jnp.sum(p, axis=1, keepdims=True)
        o_h = jnp.dot(p.astype(BF16), vc_ref[:, pr * LANES:(pr + 1) * LANES],
                      preferred_element_type=F32) / l
        if h % 2 == 0:
            o_even = o_h
        else:
            lane = lax.broadcasted_iota(I32, (tq, LANES), 1)
            o_ref[:, pr * LANES:(pr + 1) * LANES] = jnp.where(lane < 64, o_even, o_h).astype(BF16)


def _attn_c(cq, ck, cv, bias, *, nb, s_q, l_keys, tq, off):
    nq = s_q // tq
    nkblk = l_keys // C_KEY_BLOCK
    kern = functools.partial(_attn_c_kernel, tq=tq, off=off)
    qmap = lambda b, i: (b * nq + i, 0)

    qblk = tq // C_KEY_BLOCK

    def kspec(j):
        return pl.BlockSpec((C_KEY_BLOCK, HW),
                            lambda b, i: (b * nkblk + jnp.maximum(i * qblk + j - off, 0), 0))

    kv_specs = [kspec(j) for j in range(C_BAND_BLOCKS)]
    return pl.pallas_call(
        kern, grid=(nb, nq),
        in_specs=[pl.BlockSpec((tq, HW), qmap)] + kv_specs + kv_specs + [_resident(bias)],
        out_specs=pl.BlockSpec((tq, HW), qmap),
        out_shape=jax.ShapeDtypeStruct((nb * s_q, HW), BF16),
        scratch_shapes=[pltpu.VMEM((C_BAND, HW), BF16), pltpu.VMEM((C_BAND, HW), BF16)],
        compiler_params=_params(2), name=f"attn_c_q{s_q}")(
            cq, *([ck] * C_BAND_BLOCKS), *([cv] * C_BAND_BLOCKS), bias)


def _merge_kernel(x_ref, g1_ref, oa_ref, ob_ref, oc_ref, wg_ref, woa_ref, wob_ref, woc_ref, wout_ref, y_ref):
    x = x_ref[...]
    xn = _rms(x, g1_ref[...]).astype(BF16)
    merged = jnp.zeros(x.shape, F32)
    for b, (o_ref, w_ref) in enumerate(((oa_ref, woa_ref), (ob_ref, wob_ref), (oc_ref, woc_ref))):
        gate = jax.nn.sigmoid(jnp.dot(xn, wg_ref[:, b * D_MODEL:(b + 1) * D_MODEL],
                                      preferred_element_type=F32))
        merged = merged + gate * jnp.dot(o_ref[...], w_ref[...], preferred_element_type=F32)
    y_ref[...] = x + jnp.dot(merged.astype(BF16), wout_ref[...], preferred_element_type=F32)


def _mlp_kernel(x_ref, g2_ref, wup_ref, wdn_ref, gf_ref, y_ref, *, ff_chunk, final):
    x = x_ref[...]
    xn = _rms(x, g2_ref[...]).astype(BF16)
    acc = jnp.zeros(x.shape, F32)
    for c in range(D_FF // ff_chunk):
        h = jnp.dot(xn, wup_ref[:, c * ff_chunk:(c + 1) * ff_chunk], preferred_element_type=F32)
        h = jnp.square(jnp.maximum(h, 0.0)).astype(BF16)
        acc = acc + jnp.dot(h, wdn_ref[c * ff_chunk:(c + 1) * ff_chunk, :], preferred_element_type=F32)
    y = x + acc
    if final:
        y = _rms(y, gf_ref[...])
    y_ref[...] = y


def _rope_tables(pos, rot, period, live_lanes):
    half = rot // 2
    n = pos.shape[0]
    inv_freq = ROPE_THETA ** (-jnp.arange(half, dtype=F32) / half)
    ang = pos.astype(F32)[:, None] * inv_freq[None, :]
    cos, sin = jnp.cos(ang), jnp.sin(ang)
    reps = live_lanes // period
    dead = LANES - reps * period
    cos_p = jnp.concatenate([cos, cos, jnp.ones((n, period - rot), F32)], axis=1)
    sin_p = jnp.concatenate([-sin, sin, jnp.zeros((n, period - rot), F32)], axis=1)
    cos_t = jnp.concatenate([cos_p] * reps + [jnp.ones((n, dead), F32)], axis=1)
    sin_t = jnp.concatenate([sin_p] * reps + [jnp.zeros((n, dead), F32)], axis=1)
    return cos_t, sin_t


def _layer_weights(w_in, a_w_uq, a_w_uk, a_w_uv):
    offs = np.cumsum([0, A_Q_LORA, A_KV_LORA, A_ROPE, HW, HW, HW, HW, IDX_DIM, HEADS, HW, HW, HW,
                      3 * D_MODEL])
    seg = lambda i: w_in[:, offs[i]:offs[i + 1]]
    zeros = lambda n: jnp.zeros((D_MODEL, n), w_in.dtype)
    w_a = jnp.concatenate([seg(0), seg(1), seg(2), zeros(LANES - A_ROPE)], axis=1)
    w_b = jnp.concatenate([seg(3), seg(4), seg(5), seg(6), seg(7), zeros(LANES - IDX_DIM),
                           seg(8), zeros(LANES - HEADS)], axis=1)
    w_c = jnp.concatenate([seg(9), seg(10), seg(11)], axis=1)
    w_g = seg(12)
    uq = a_w_uq.reshape(A_Q_LORA, HEADS, A_NOPE + A_ROPE)
    uq_nope = uq[:, :, :A_NOPE].reshape(A_Q_LORA, HW)
    uq_rope = jnp.pad(uq[:, :, A_NOPE:], ((0, 0), (0, 0), (0, LANES - A_ROPE))).reshape(A_Q_LORA, HEADS * LANES)
    w_uq = jnp.concatenate([uq_nope, uq_rope], axis=1)
    uk = a_w_uk.reshape(A_KV_LORA, HEADS, A_NOPE)
    ukt = jnp.transpose(uk, (1, 2, 0))
    ukt_pad = jnp.stack([jnp.pad(ukt[h], ((0, 64), (0, 0)) if h % 2 == 0 else ((64, 0), (0, 0)))
                         for h in range(HEADS)], axis=0)
    uv = a_w_uv.reshape(A_KV_LORA, HEADS, A_VDIM)
    uv_pad = jnp.stack([jnp.pad(uv[:, h], ((0, 0), (h * A_VDIM, HW - (h + 1) * A_VDIM)))
                        for h in range(HEADS)], axis=0)
    bf = lambda a: a.astype(BF16)
    return bf(w_a), bf(w_b), bf(w_c), bf(w_g), bf(w_uq), bf(ukt_pad), bf(uv_pad)


def _band_bias(rel_bias, q_pos, k_pos, k_live):
    q_c, k_c = q_pos // CHUNK, k_pos // CHUNK
    vis = (k_c[None, :] <= q_c[:, None]) & (k_c[None, :] >= q_c[:, None] - C_LEFT_CHUNKS) & k_live[None, :]
    nq, nk = len(q_pos), len(k_pos)
    diag = (q_pos[0] - k_pos[0]) + (nq - 1) - np.arange(nq + nk - 1)
    ext = rel_bias[:, np.clip(diag, -C_REL_CLIP, C_REL_CLIP) + C_REL_CLIP].astype(F32) * LOG2E
    bias = jnp.stack([ext[:, nq - 1 - i:nq - 1 - i + nk] for i in range(nq)], axis=1)
    return jnp.where(vis[None], bias, NEG_INF)


def _pad_rows(a, n):
    return jnp.pad(a, ((0, 0), (0, n - a.shape[1]), (0, 0)))


def _mixers(x2d, lw, *, nb, s_q, q_pos0, past, tm, tq_a, tq_b, tq_c, tk_a, tk_b, tables, c_rel_bias):
    (g1, gq, gkv, w_a, w_b, w_c, w_uq, ukt_pad, uv_pad) = lw
    cos_a, sin_a, cos_b, sin_b, cos_k, sin_k = tables
    t = nb * s_q
    res = _resident
    qa, ka, ckv32, kr32 = _row_call(
        _prep_a_kernel, x2d, tm, (g1, w_a, gq, gkv, w_uq, ukt_pad, cos_a, sin_a),
        [res(g1), res(w_a), res(gq), res(gkv), res(w_uq), res(ukt_pad), _table_spec(cos_a, tm),
         _table_spec(sin_a, tm)],
        [(HEADS * 256, BF16), (256, BF16), (A_KV_LORA, F32), (A_ROPE, F32)], "prep_a")
    bq, bk, bv, iq, ik2, iw, bk32, bv32, ik32 = _row_call(
        _prep_b_kernel, x2d, tm, (g1, w_b, cos_b, sin_b, cos_k, sin_k),
        [res(g1), res(w_b), _table_spec(cos_b, tm), _table_spec(sin_b, tm), _table_spec(cos_k, tm),
         _table_spec(sin_k, tm)],
        [(HW, BF16), (HW, BF16), (HW, BF16), (HW, BF16), (LANES, BF16), (LANES, F32),
         (HW, F32), (HW, F32), (IDX_DIM, F32)], "prep_b")
    cq, ck, cv = _row_call(_prep_c_kernel, x2d, tm, (g1, w_c), [res(g1), res(w_c)],
                           [(HW, BF16), (HW, BF16), (HW, BF16)], "prep_c")

    if past is None:
        l_valid = s_q
        l_keys = s_q
        keys_a, keys_bk, keys_bv, keys_ik = ka, bk, bv, ik2
        keys_ck, keys_cv = ck, cv
        lc_keys = s_q
        c_off = C_REACH // C_KEY_BLOCK
        assert c_off + tq_c // C_KEY_BLOCK == C_BAND_BLOCKS
        qp = np.arange(tq_c)
        kp = np.arange(C_BAND) - C_REACH
        bias = _band_bias(c_rel_bias, qp, kp, np.ones(C_BAND, bool))
        keep = min(C_REACH, s_q)
        x_tail = x2d.reshape(nb, s_q, D_MODEL)[:, s_q - keep:].reshape(nb * keep, D_MODEL)
        w_ckv = w_c[:, HW:]
        ck32, cv32 = _row_call(_prep_c32_kernel, x_tail, min(tm, nb * keep), (g1, w_ckv),
                               [res(g1), res(w_ckv)], [(HW, F32), (HW, F32)], "prep_c32")
        ck32 = ck32.reshape(nb, keep, HEADS, C_HEAD_DIM)
        cv32 = cv32.reshape(nb, keep, HEADS, C_HEAD_DIM)
    else:
        p_ckv, p_kr, p_bk, p_bv, p_ik, p_ck, p_cv = past
        past_len = p_ckv.shape[1]
        l_valid = past_len + s_q
        l_keys = -(-l_valid // LANES) * LANES
        r3 = lambda a: a.reshape(nb, s_q, a.shape[-1])
        cat = lambda old, new: _pad_rows(jnp.concatenate([old.astype(BF16), r3(new)], axis=1), l_keys)
        flat = lambda a: a.reshape(nb * a.shape[1], a.shape[2])
        old_a = jnp.concatenate([p_ckv, p_kr, jnp.zeros((nb, past_len, 256 - A_KV_LORA - A_ROPE), F32)], axis=-1)
        keys_a = flat(cat(old_a, ka))
        keys_bk = flat(cat(p_bk.reshape(nb, past_len, HW), bk))
        keys_bv = flat(cat(p_bv.reshape(nb, past_len, HW), bv))
        keys_ik = flat(cat(jnp.concatenate([p_ik, p_ik], axis=-1), ik2))
        w_c_len = p_ck.shape[1]
        lc_valid = w_c_len + s_q
        lc_keys = C_BAND
        catc = lambda old, new: _pad_rows(jnp.concatenate([old.astype(BF16), r3(new)], axis=1), lc_keys)
        keys_ck = flat(catc(p_ck.reshape(nb, w_c_len, HW), ck))
        keys_cv = flat(catc(p_cv.reshape(nb, w_c_len, HW), cv))
        c_off = 0
        qp = np.arange(q_pos0, q_pos0 + s_q)
        kp = np.arange(past_len - w_c_len, past_len - w_c_len + lc_keys)
        bias = _band_bias(c_rel_bias, qp, kp, np.arange(lc_keys) < lc_valid)
        ck32, cv32 = _row_call(_prep_c32_kernel, x2d, tm, (g1, w_c[:, HW:]),
                               [res(g1), res(w_c[:, HW:])], [(HW, F32), (HW, F32)], "prep_c32")
        ck32 = ck32.reshape(nb, s_q, HEADS, C_HEAD_DIM)
        cv32 = cv32.reshape(nb, s_q, HEADS, C_HEAD_DIM)

    topk = min(B_TOPK_MAX, l_valid // 4)
    tk_a = min(tk_a, l_keys)
    tk_b = min(tk_b, l_keys)
    o_a = _attn_a(qa, keys_a, uv_pad, nb=nb, s_q=s_q, l_keys=l_keys, tq=tq_a, tk=tk_a,
                  q_pos0=q_pos0, l_valid=l_valid)
    o_b = _attn_b(bq, iq, iw, keys_bk, keys_bv, keys_ik, nb=nb, s_q=s_q, l_keys=l_keys, tq=tq_b,
                  tk=tk_b, q_pos0=q_pos0, l_valid=l_valid, topk=topk)
    o_c = _attn_c(cq, keys_ck, keys_cv, bias, nb=nb, s_q=s_q, l_keys=lc_keys, tq=tq_c, off=c_off)
    new_rows = (ckv32.reshape(nb, s_q, A_KV_LORA), kr32.reshape(nb, s_q, A_ROPE),
                bk32.reshape(nb, s_q, HEADS, B_HEAD_DIM), bv32.reshape(nb, s_q, HEADS, B_HEAD_DIM),
                ik32.reshape(nb, s_q, IDX_DIM), ck32, cv32)
    return o_a, o_b, o_c, new_rows


def _layer(x2d, lw_mix, lw_rest, gf, *, final, tm, **kw):
    o_a, o_b, o_c, new_rows = _mixers(x2d, lw_mix, tm=tm, **kw)
    g1 = lw_mix[0]
    (w_g, w_oa, w_ob, w_oc, w_out, g2, w_up, w_dn) = lw_rest
    res = _resident
    row = lambda w: pl.BlockSpec((tm, w), lambda i: (i, 0))
    (x1,) = _row_call(_merge_kernel, x2d, tm, (g1, o_a, o_b, o_c, w_g, w_oa, w_ob, w_oc, w_out),
                      [res(g1), row(HW), row(HW), row(HW), res(w_g), res(w_oa), res(w_ob), res(w_oc),
                       res(w_out)], [(D_MODEL, F32)], "merge")
    (x2,) = _row_call(functools.partial(_mlp_kernel, ff_chunk=1024, final=final), x1, tm,
                      (g2, w_up, w_dn, gf), [res(g2), res(w_up), res(w_dn), res(gf)], [(D_MODEL, F32)],
                      "mlp_final" if final else "mlp")
    return x2, new_rows


def _tiles(nb, s_q, prompt):
    if prompt:
        tm = 512 if (nb * s_q) % 512 == 0 else 256
        return dict(tm=tm, tq_a=128, tq_b=256, tq_c=256, tk_a=512, tk_b=512)
    whole = 1 << 30
    return dict(tm=nb * s_q, tq_a=s_q, tq_b=s_q, tq_c=s_q, tk_a=whole, tk_b=whole)


def kernel(x_prompt, x_sample, cache_a_ckv, cache_a_krope, cache_b_k, cache_b_v, cache_b_idx_k, cache_c_k, cache_c_v, norm1, w_in, a_q_norm, a_kv_norm, a_w_uq, a_w_uk, a_w_uv, c_rel_bias, w_oa, w_ob, w_oc, w_out, norm2, w_up, w_down, final_norm):
    depth = w_in.shape[0]
    nb_p, s_p, _ = x_prompt.shape
    nb_s, s_s, _ = x_sample.shape
    past_len = cache_a_ckv.shape[2]
    assert s_p % 256 == 0 and CHUNK % s_s == 0 and past_len % CHUNK == 0
    tiles_p, tiles_s = _tiles(nb_p, s_p, prompt=True), _tiles(nb_s, s_s, prompt=False)

    pos_p = jnp.arange(s_p, dtype=jnp.int32)
    pos_s = jnp.tile(jnp.arange(past_len, past_len + s_s, dtype=jnp.int32), nb_s)

    def tables(pos):
        return (_rope_tables(pos, A_ROPE, A_ROPE, A_ROPE) + _rope_tables(pos, B_ROT, B_HEAD_DIM, LANES)
                + _rope_tables(pos, IDX_ROT, IDX_DIM, IDX_DIM))

    tab_p, tab_s = tables(pos_p), tables(pos_s)
    gf = final_norm.reshape(1, D_MODEL)
    bf = lambda a: a.astype(BF16)

    xp = x_prompt.reshape(nb_p * s_p, D_MODEL)
    xs = x_sample.reshape(nb_s * s_s, D_MODEL)
    rows_p, rows_s = [], []
    for l in range(depth):
        w_a, w_b, w_c, w_g, w_uq, ukt_pad, uv_pad = _layer_weights(w_in[l], a_w_uq[l], a_w_uk[l], a_w_uv[l])
        lw_mix = (norm1[l].reshape(1, -1), a_q_norm[l].reshape(1, -1), a_kv_norm[l].reshape(1, -1),
                  w_a, w_b, w_c, w_uq, ukt_pad, uv_pad)
        lw_rest = (w_g, bf(w_oa[l]), bf(w_ob[l]), bf(w_oc[l]), bf(w_out[l]), norm2[l].reshape(1, -1),
                   bf(w_up[l]), bf(w_down[l]))
        final = l == depth - 1
        past = (cache_a_ckv[l], cache_a_krope[l], cache_b_k[l], cache_b_v[l], cache_b_idx_k[l],
                cache_c_k[l], cache_c_v[l])
        xp, new_p = _layer(xp, lw_mix, lw_rest, gf, final=final, nb=nb_p, s_q=s_p, q_pos0=0,
                           past=None, tables=tab_p, c_rel_bias=c_rel_bias[l], **tiles_p)
        xs, new_s = _layer(xs, lw_mix, lw_rest, gf, final=final, nb=nb_s, s_q=s_s, q_pos0=past_len,
                           past=past, tables=tab_s, c_rel_bias=c_rel_bias[l], **tiles_s)
        rows_p.append(new_p)
        rows_s.append(new_s)

    y_prompt = xp.reshape(nb_p, s_p, D_MODEL)
    y_sample = xs.reshape(nb_s, s_s, D_MODEL)
    stack = lambda rows, i: jnp.stack([r[i] for r in rows], axis=0)
    outs = [y_prompt, y_sample]
    for i in range(7):
        outs += [stack(rows_p, i), stack(rows_s, i)]
    return tuple(outs)
```

```python
import functools

import numpy as np
import jax
import jax.numpy as jnp
from jax import lax
from jax.experimental import pallas as pl
from jax.experimental.pallas import tpu as pltpu

F32 = jnp.float32
BF16 = jnp.bfloat16
I32 = jnp.int32

D_MODEL = 1024
CHUNK = 64
ROPE_THETA = 500000.0
RMS_EPS = 1e-6
HEADS = 8
A_Q_LORA = 256
A_KV_LORA = 128
A_NOPE = 64
A_ROPE = 32
A_VDIM = 64
A_SCALE = (A_NOPE + A_ROPE) ** -0.5
B_HEAD_DIM = 64
B_ROT = B_HEAD_DIM // 4
B_SCALE = B_HEAD_DIM ** -0.5
B_TOPK_MAX = 256
IDX_DIM = 64
IDX_ROT = IDX_DIM // 4
C_HEAD_DIM = 64
C_SCALE = C_HEAD_DIM ** -0.5
C_LEFT_CHUNKS = 8
C_REACH = C_LEFT_CHUNKS * CHUNK
C_REL_CLIP = 128
D_FF = 4 * D_MODEL
HW = HEADS * 64
LANES = 128
C_KEY_BLOCK = 128
C_BAND_BLOCKS = 6
C_BAND = C_KEY_BLOCK * C_BAND_BLOCKS
VMEM_LIMIT = 56 * 1024 * 1024
NEG_INF = float("-inf")
LOG2E = 1.4426950408889634
INT_MIN = -2 ** 31


def _rms(x, g):
    ms = jnp.mean(x * x, axis=-1, keepdims=True)
    return x * lax.rsqrt(ms + RMS_EPS) * g


def _rope_lanes(x, cos, sin, half):
    w = x.shape[-1]
    outs = []
    for j in range(w // LANES):
        xs = x[:, j * LANES:(j + 1) * LANES]
        lane = lax.broadcasted_iota(I32, xs.shape, 1)
        first = (lane % (2 * half)) < half
        partner = jnp.where(first, pltpu.roll(xs, LANES - half, 1), pltpu.roll(xs, half, 1))
        outs.append(xs * cos + partner * sin)
    return outs[0] if len(outs) == 1 else jnp.concatenate(outs, axis=1)


def _head_half_mask(shape, head):
    lane = lax.broadcasted_iota(I32, shape, 1)
    return (lane < 64) if head % 2 == 0 else (lane >= 64)


def _resident(arr):
    nd = arr.ndim
    return pl.BlockSpec(arr.shape, lambda *_: (0,) * nd, pipeline_mode=pl.Buffered(1))


def _params(n_axes):
    return pltpu.CompilerParams(dimension_semantics=("arbitrary",) * n_axes,
                                vmem_limit_bytes=VMEM_LIMIT)


def _prep_a_kernel(x_ref, g1_ref, w_ref, gq_ref, gkv_ref, wuq_ref, wukt_ref, cos_ref, sin_ref,
                   qa_ref, ka_ref, ckv_ref, kr_ref):
    xn = _rms(x_ref[...], g1_ref[...]).astype(BF16)
    z = jnp.dot(xn, w_ref[...], preferred_element_type=F32)
    aq = z[:, :A_Q_LORA]
    akv = z[:, A_Q_LORA:A_Q_LORA + A_KV_LORA]
    akr = z[:, A_Q_LORA + A_KV_LORA:]
    cos = cos_ref[...]
    sin = sin_ref[...]
    ckv = _rms(akv, gkv_ref[...])
    ckv_ref[...] = ckv
    kr = _rope_lanes(akr, cos, sin, A_ROPE // 2)
    kr_ref[...] = kr[:, :A_ROPE]
    ka_ref[...] = jnp.concatenate([ckv, kr], axis=1).astype(BF16)
    aqn = _rms(aq, gq_ref[...]).astype(BF16)
    qa = jnp.dot(aqn, wuq_ref[...], preferred_element_type=F32)
    for h in range(HEADS):
        pair = qa[:, (h // 2) * LANES:(h // 2 + 1) * LANES].astype(BF16)
        q_lat = jnp.dot(pair, wukt_ref[h], preferred_element_type=F32)
        q_rope = _rope_lanes(qa[:, HW + h * LANES:HW + (h + 1) * LANES], cos, sin, A_ROPE // 2)
        qa_ref[:, h * 256:(h + 1) * 256] = (
            jnp.concatenate([q_lat, q_rope], axis=1) * (A_SCALE * LOG2E)).astype(BF16)


def _prep_b_kernel(x_ref, g1_ref, w_ref, cos_ref, sin_ref, cosk_ref, sink_ref,
                   bq_ref, bk_ref, bv_ref, iq_ref, ik2_ref, iw_ref, bk32_ref, bv32_ref, ik32_ref):
    xn = _rms(x_ref[...], g1_ref[...]).astype(BF16)
    z = jnp.dot(xn, w_ref[...], preferred_element_type=F32)
    cos = cos_ref[...]
    sin = sin_ref[...]
    bq = _rope_lanes(z[:, 0:HW], cos, sin, B_ROT // 2)
    bq_ref[...] = (bq * (B_SCALE * LOG2E)).astype(BF16)
    bk = _rope_lanes(z[:, HW:2 * HW], cos, sin, B_ROT // 2)
    bk32_ref[...] = bk
    bk_ref[...] = bk.astype(BF16)
    bv = z[:, 2 * HW:3 * HW]
    bv32_ref[...] = bv
    bv_ref[...] = bv.astype(BF16)
    iq = _rope_lanes(z[:, 3 * HW:4 * HW], cos, sin, IDX_ROT // 2)
    iq_ref[...] = (iq * IDX_DIM ** -0.5).astype(BF16)
    ik = _rope_lanes(z[:, 4 * HW:4 * HW + LANES], cosk_ref[...], sink_ref[...], IDX_ROT // 2)
    ik32_ref[...] = ik[:, :IDX_DIM]
    lane = lax.broadcasted_iota(I32, ik.shape, 1)
    ik2_ref[...] = jnp.where(lane < IDX_DIM, ik, pltpu.roll(ik, IDX_DIM, 1)).astype(BF16)
    iw_ref[...] = z[:, 4 * HW + LANES:] * HEADS ** -0.5


def _prep_c_kernel(x_ref, g1_ref, w_ref, cq_ref, ck_ref, cv_ref):
    xn = _rms(x_ref[...], g1_ref[...]).astype(BF16)
    z = jnp.dot(xn, w_ref[...], preferred_element_type=F32)
    cq_ref[...] = (z[:, 0:HW] * (C_SCALE * LOG2E)).astype(BF16)
    ck_ref[...] = z[:, HW:2 * HW].astype(BF16)
    cv_ref[...] = z[:, 2 * HW:].astype(BF16)


def _prep_c32_kernel(x_ref, g1_ref, w_ref, ck_ref, cv_ref):
    xn = _rms(x_ref[...], g1_ref[...]).astype(BF16)
    z = jnp.dot(xn, w_ref[...], preferred_element_type=F32)
    ck_ref[...] = z[:, 0:HW]
    cv_ref[...] = z[:, HW:]


def _row_call(kernel, x, tm, operands, specs, outs, name):
    t = x.shape[0]
    name = f"{name}_t{t}"
    grid = (t // tm,)
    in_specs = [pl.BlockSpec((tm, x.shape[1]), lambda i: (i, 0))] + specs
    out_shape = [jax.ShapeDtypeStruct((t, w), dt) for w, dt in outs]
    out_specs = [pl.BlockSpec((tm, w), lambda i: (i, 0)) for w, _ in outs]
    return pl.pallas_call(kernel, grid=grid, in_specs=in_specs, out_specs=out_specs,
                          out_shape=out_shape, compiler_params=_params(1), name=name)(x, *operands)


def _table_spec(table, tm):
    nblk = table.shape[0] // tm
    return pl.BlockSpec((tm, table.shape[1]), lambda i: (i % nblk, 0))


def _attn_a_kernel(q_ref, k_ref, wuv_ref, o_ref, m_ref, acc_ref, *, tq, tk, q_pos0, l_valid):
    qi = pl.program_id(1)
    qs = jnp.concatenate([q_ref[:, h * 256:(h + 1) * 256] for h in range(HEADS)], axis=0)
    qpos0 = q_pos0 + qi * tq
    qrow = lax.rem(lax.broadcasted_iota(I32, (HEADS * tq, 1), 0), tq)
    limit_row = jnp.minimum(((qpos0 + qrow) // CHUNK + 1) * CHUNK, l_valid)
    limit_max = jnp.minimum(((qpos0 + tq - 1) // CHUNK + 1) * CHUNK, l_valid)
    nkb = (limit_max + tk - 1) // tk
    m_ref[...] = jnp.full(m_ref.shape, NEG_INF, F32)
    acc_ref[...] = jnp.zeros(acc_ref.shape, F32)
    v_ones = jnp.ones((tk, LANES), BF16)

    def scores(kb):
        start = pl.multiple_of(kb * tk, tk)
        return lax.dot_general(qs, k_ref[pl.ds(start, tk), :], (((1,), (1,)), ((), ())),
                               preferred_element_type=F32)

    def update(kb, s, masked):
        start = pl.multiple_of(kb * tk, tk)
        if masked:
            kpos = start + lax.broadcasted_iota(I32, (1, tk), 1)
            s = jnp.where(kpos < limit_row, s, NEG_INF)
        mx = s[:, 0:LANES]
        for j in range(1, tk // LANES):
            mx = jnp.maximum(mx, s[:, j * LANES:(j + 1) * LANES])
        m_old = m_ref[...]
        m_new = jnp.maximum(m_old, jnp.broadcast_to(jnp.max(mx, axis=1, keepdims=True), m_old.shape))
        alpha = jnp.exp2(m_old - m_new)
        p = jnp.concatenate([jnp.exp2(s[:, j * LANES:(j + 1) * LANES] - m_new)
                             for j in range(tk // LANES)], axis=1).astype(BF16)
        vext = jnp.concatenate([k_ref[pl.ds(start, tk), :A_KV_LORA], v_ones], axis=1)
        acc_ref[...] = jnp.concatenate([alpha, alpha], axis=1) * acc_ref[...] + jnp.dot(
            p, vext, preferred_element_type=F32)
        m_ref[...] = m_new

    def body(kb, s):
        s_next = scores(kb + 1)
        update(kb, s, False)
        return s_next

    s_last = lax.fori_loop(0, nkb - 1, body, scores(0))
    update(nkb - 1, s_last, True)
    acc = acc_ref[...]
    o_lat = (acc[:, :A_KV_LORA] / acc[:, A_KV_LORA:]).astype(BF16)
    o = jnp.zeros((tq, HW), F32)
    for h in range(HEADS):
        o = o + jnp.dot(o_lat[h * tq:(h + 1) * tq], wuv_ref[h], preferred_element_type=F32)
    o_ref[...] = o.astype(BF16)


def _attn_a(qa, ka, wuv_pad, *, nb, s_q, l_keys, tq, tk, q_pos0, l_valid):
    nq = s_q // tq
    assert tq <= 2 * CHUNK <= tk and (tq <= CHUNK or q_pos0 % tq == 0)
    kern = functools.partial(_attn_a_kernel, tq=tq, tk=tk, q_pos0=q_pos0, l_valid=l_valid)
    return pl.pallas_call(
        kern, grid=(nb, nq),
        in_specs=[pl.BlockSpec((tq, HEADS * 256), lambda b, i: (b * nq + i, 0)),
                  pl.BlockSpec((l_keys, 256), lambda b, i: (b, 0), pipeline_mode=pl.Buffered(1)),
                  _resident(wuv_pad)],
        out_specs=pl.BlockSpec((tq, HW), lambda b, i: (b * nq + i, 0)),
        out_shape=jax.ShapeDtypeStruct((nb * s_q, HW), BF16),
        scratch_shapes=[pltpu.VMEM((HEADS * tq, LANES), F32), pltpu.VMEM((HEADS * tq, 2 * LANES), F32)],
        compiler_params=_params(2), name=f"attn_a_q{s_q}")(qa, ka, wuv_pad)


def _sortable_key(x):
    bits = lax.bitcast_convert_type(x, I32)
    return bits ^ ((bits >> 31) & 0x7FFFFFFF)


def _attn_b_kernel(q_ref, iq_ref, iw_ref, k_ref, v_ref, ik_ref, o_ref, s_ref, bias_ref, m_ref, acc_ref,
                   *, tq, tk, q_pos0, l_valid, topk, idx_bits, group):
    qi = pl.program_id(1)
    qpos0 = q_pos0 + qi * tq
    row = lax.broadcasted_iota(I32, (tq, 1), 0)
    limit_row = jnp.minimum(((qpos0 + row) // CHUNK + 1) * CHUNK, l_valid)
    limit_max = jnp.minimum(((qpos0 + tq - 1) // CHUNK + 1) * CHUNK, l_valid)
    nkb = (limit_max + tk - 1) // tk
    kf = float(topk)

    iq = iq_ref[...]
    iqs = jnp.concatenate(
        [jnp.where(_head_half_mask((tq, LANES), h), iq[:, (h // 2) * LANES:(h // 2 + 1) * LANES],
                   jnp.zeros((tq, LANES), BF16)) for h in range(HEADS)], axis=0)
    w = iw_ref[...]

    def score_body(kb, carry):
        start = pl.multiple_of(kb * tk, tk)
        x = lax.dot_general(iqs, ik_ref[pl.ds(start, tk), :], (((1,), (1,)), ((), ())),
                            preferred_element_type=F32)
        sc = jnp.zeros((tq, tk), F32)
        for h in range(HEADS):
            sc = sc + jnp.maximum(x[h * tq:(h + 1) * tq], 0.0) * w[:, h:h + 1]
        kpos = start + lax.broadcasted_iota(I32, (1, tk), 1)
        sc = jnp.where(kpos < limit_row, sc + 0.0, NEG_INF)
        s_ref[kb] = _sortable_key(sc)
        return carry

    nkg = (nkb + group - 1) // group
    nsl = tk // LANES
    if tq % LANES == 0:
        _select_keys_on_sublanes(iqs, iw_ref, ik_ref, s_ref, bias_ref, tq=tq, tk=tk, qpos0=qpos0,
                                 l_valid=l_valid, kf=kf, idx_bits=idx_bits, group=group, nkb=nkb, nkg=nkg)
    else:
        lax.fori_loop(0, nkg * group, score_body, 0)
        _select_keys_on_lanes(s_ref, bias_ref, limit_row, tq=tq, tk=tk, kf=kf, idx_bits=idx_bits,
                              group=group, nkb=nkb, nkg=nkg)
    _masked_attention(q_ref, k_ref, v_ref, bias_ref, o_ref, m_ref, acc_ref, tq=tq, tk=tk, nkb=nkb)


def _select_keys_on_lanes(s_ref, bias_ref, limit_row, *, tq, tk, kf, idx_bits, group, nkb, nkg):
    nsl = tk // LANES
    ones = jnp.ones((LANES, LANES), BF16)
    limit_rep = jnp.broadcast_to(limit_row, (tq, LANES))
    rh = min(tq, LANES)

    def count(pred):
        parts = []
        for r in range(tq // rh):
            rows = slice(r * rh, (r + 1) * rh)

            def body(g, c, rows=rows):
                for u in range(group):
                    kb = g * group + u
                    for j in range(nsl):
                        blk = s_ref[kb, rows, j * LANES:(j + 1) * LANES]
                        kpos = kb * tk + j * LANES + lax.broadcasted_iota(I32, (1, LANES), 1)
                        c = c + jnp.where(pred(blk, kpos, rows), 1.0, 0.0)
                return c

            parts.append(lax.fori_loop(0, nkg, body, jnp.zeros((rh, LANES), F32)))
        c = parts[0] if len(parts) == 1 else jnp.concatenate(parts, axis=0)
        return jnp.dot(c.astype(BF16), ones, preferred_element_type=F32)

    def row_max():
        parts = []
        for r in range(tq // rh):
            rows = slice(r * rh, (r + 1) * rh)

            def body(g, mx, rows=rows):
                for u in range(group):
                    for j in range(nsl):
                        mx = jnp.maximum(mx, s_ref[g * group + u, rows, j * LANES:(j + 1) * LANES])
                return mx

            parts.append(lax.fori_loop(0, nkg, body, jnp.full((rh, LANES), INT_MIN, I32)))
        mx = parts[0] if len(parts) == 1 else jnp.concatenate(parts, axis=0)
        return jnp.broadcast_to(jnp.max(mx, axis=1, keepdims=True), (tq, LANES))

    key_max = row_max()
    n_all = (nkg * group * tk).astype(F32)
    n_pos = count(lambda blk, kpos, rows: blk >= 0)
    thr0 = jnp.where(n_pos >= kf, 0, INT_MIN).astype(I32)
    n_thr0 = jnp.where(n_pos >= kf, n_pos, n_all)

    def all_rows(flag):
        return jnp.min(jnp.where(flag, 1.0, 0.0)) > 0.5

    def bit_cond(state):
        i, resolved, _, _ = state
        return (i < 31) & (resolved == 0)

    def bit_body(state):
        i, _, t, n_t = state
        cand = t + (jnp.int32(1) << (30 - i))
        c = lax.cond(all_rows(cand > key_max), lambda: jnp.zeros((tq, LANES), F32),
                     lambda: count(lambda blk, kpos, rows: blk >= cand[rows]))
        take = c >= kf
        n_t = jnp.where(take, c, n_t)
        return i + 1, all_rows(n_t == kf).astype(I32), jnp.where(take, cand, t), n_t

    _, _, thr, n_ge = lax.while_loop(
        bit_cond, bit_body, (jnp.int32(0), all_rows(n_thr0 == kf).astype(I32), thr0, n_thr0))
    excess = jnp.max(jnp.where(n_ge > kf, 1.0, 0.0)) > 0.0

    def tie_cut(_):
        need = kf - count(lambda blk, kpos, rows: blk > thr[rows])

        def jb(i, j):
            cand = j + (jnp.int32(1) << (idx_bits - 1 - i))
            c = count(lambda blk, kpos, rows: (blk == thr[rows]) & (kpos < cand[rows]))
            return jnp.where(c < need, cand, j)
        return lax.fori_loop(0, idx_bits, jb, jnp.zeros((tq, LANES), I32))

    cut = lax.cond(excess, tie_cut, lambda _: jnp.full((tq, LANES), 2 ** 30, I32), 0)

    def bias_body(kb, carry):
        for j in range(nsl):
            blk = s_ref[kb, :, j * LANES:(j + 1) * LANES]
            kpos = kb * tk + j * LANES + lax.broadcasted_iota(I32, (1, LANES), 1)
            sel = ((blk > thr) | ((blk == thr) & (kpos <= cut))) & (kpos < limit_rep)
            bias_ref[kb, :, j * LANES:(j + 1) * LANES] = jnp.where(sel, 0.0, NEG_INF).astype(F32)
        return carry

    lax.fori_loop(0, nkb, bias_body, 0)


def _masked_attention(q_ref, k_ref, v_ref, bias_ref, o_ref, m_ref, acc_ref, *, tq, tk, nkb):
    nsl = tk // LANES
    m_ref[...] = jnp.full(m_ref.shape, NEG_INF, F32)
    acc_ref[...] = jnp.zeros(acc_ref.shape, F32)
    v_ones = jnp.ones((tk, LANES), BF16)

    def attn_body(kb, carry):
        start = pl.multiple_of(kb * tk, tk)
        bias = bias_ref[kb]
        for pr in range(HEADS // 2):
            kblk = k_ref[pl.ds(start, tk), pr * LANES:(pr + 1) * LANES]
            vext = jnp.concatenate([v_ref[pl.ds(start, tk), pr * LANES:(pr + 1) * LANES], v_ones], axis=1)
            qpair = q_ref[:, pr * LANES:(pr + 1) * LANES]
            for h in (2 * pr, 2 * pr + 1):
                qm = jnp.where(_head_half_mask((tq, LANES), h), qpair, jnp.zeros((tq, LANES), BF16))
                s = lax.dot_general(qm, kblk, (((1,), (1,)), ((), ())), preferred_element_type=F32) + bias
                mx = s[:, 0:LANES]
                for j in range(1, nsl):
                    mx = jnp.maximum(mx, s[:, j * LANES:(j + 1) * LANES])
                m_old = m_ref[h]
                m_new = jnp.maximum(m_old, jnp.broadcast_to(jnp.max(mx, axis=1, keepdims=True), (tq, LANES)))
                m_safe = jnp.where(m_new == NEG_INF, 0.0, m_new)
                alpha = jnp.exp2(m_old - m_safe)
                p = jnp.concatenate([jnp.exp2(s[:, j * LANES:(j + 1) * LANES] - m_safe)
                                     for j in range(nsl)], axis=1).astype(BF16)
                acc_ref[h] = jnp.concatenate([alpha, alpha], axis=1) * acc_ref[h] + jnp.dot(
                    p, vext, preferred_element_type=F32)
                m_ref[h] = m_new
        return carry

    lax.fori_loop(0, nkb, attn_body, 0)
    lane = lax.broadcasted_iota(I32, (tq, LANES), 1)
    for pr in range(HEADS // 2):
        a_even, a_odd = acc_ref[2 * pr], acc_ref[2 * pr + 1]
        o_even = a_even[:, :LANES] / a_even[:, LANES:]
        o_odd = a_odd[:, :LANES] / a_odd[:, LANES:]
        o_ref[:, pr * LANES:(pr + 1) * LANES] = jnp.where(lane < 64, o_even, o_odd).astype(BF16)


def _select_keys_on_sublanes(iqs, iw_ref, ik_ref, st_ref, bias_ref, *, tq, tk, qpos0, l_valid, kf, idx_bits,
                             group, nkb, nkg):
    sub = 8
    nacc = 4
    lane_q = lax.broadcasted_iota(I32, (1, tq), 1)
    limit_t = jnp.minimum(((qpos0 + lane_q) // CHUNK + 1) * CHUNK, l_valid)
    w_t = jnp.transpose(iw_ref[...])

    def score_body(kb, carry):
        start = pl.multiple_of(kb * tk, tk)
        xt = lax.dot_general(ik_ref[pl.ds(start, tk), :], iqs, (((1,), (1,)), ((), ())),
                             preferred_element_type=F32)
        sc = jnp.zeros((tk, tq), F32)
        for h in range(HEADS):
            sc = sc + jnp.maximum(xt[:, h * tq:(h + 1) * tq], 0.0) * w_t[h:h + 1, :]
        kpos = start + lax.broadcasted_iota(I32, (tk, 1), 0)
        sc = jnp.where(kpos < limit_t, sc + 0.0, NEG_INF)
        st_ref[kb] = _sortable_key(sc)
        return carry

    lax.fori_loop(0, nkg * group, score_body, 0)

    def sweep(init, step, finish):
        def body(g, accs):
            accs = list(accs)
            for u in range(group):
                kb = g * group + u
                for j in range(tk // sub):
                    blk = st_ref[kb, j * sub:(j + 1) * sub, :]
                    kpos = kb * tk + j * sub + lax.broadcasted_iota(I32, (sub, 1), 0)
                    accs[j % nacc] = step(accs[j % nacc], blk, kpos)
            return tuple(accs)
        return finish(lax.fori_loop(0, nkg, body, tuple(init for _ in range(nacc))))

    def rep(x):
        return jnp.broadcast_to(x, (sub, tq))

    def count(pred):
        return sweep(jnp.zeros((sub, tq), F32),
                     lambda acc, blk, kpos: acc + jnp.where(pred(blk, kpos), 1.0, 0.0),
                     lambda accs: rep(jnp.sum(sum(accs[1:], accs[0]), axis=0, keepdims=True)))

    n_all = (nkg * group * tk).astype(F32)
    n_pos = count(lambda blk, kpos: blk >= 0)
    thr0 = jnp.where(n_pos >= kf, 0, INT_MIN).astype(I32)
    n_thr0 = jnp.where(n_pos >= kf, n_pos, n_all)

    def bit_body(i, state):
        t, n_t = state
        cand = t + (jnp.int32(1) << (30 - i))
        c = count(lambda blk, kpos: blk >= cand)
        take = c >= kf
        return jnp.where(take, cand, t), jnp.where(take, c, n_t)

    thr, n_ge = lax.fori_loop(0, 31, bit_body, (thr0, n_thr0))
    excess = jnp.max(jnp.where(n_ge > kf, 1.0, 0.0)) > 0.0

    def tie_cut(_):
        need = kf - count(lambda blk, kpos: blk > thr)

        def jb(i, j):
            cand = j + (jnp.int32(1) << (idx_bits - 1 - i))
            c = count(lambda blk, kpos: (blk == thr) & (kpos < cand))
            return jnp.where(c < need, cand, j)
        return lax.fori_loop(0, idx_bits, jb, jnp.zeros((sub, tq), I32))

    cut = lax.cond(excess, tie_cut, lambda _: jnp.full((sub, tq), 2 ** 30, I32), 0)
    thr_q, cut_q = thr[0:1, :], cut[0:1, :]

    def bias_body(kb, carry):
        blk = st_ref[kb]
        kpos = kb * tk + lax.broadcasted_iota(I32, (tk, 1), 0)
        sel = ((blk > thr_q) | ((blk == thr_q) & (kpos <= cut_q))) & (kpos < limit_t)
        bias_ref[kb] = jnp.transpose(jnp.where(sel, 0.0, NEG_INF).astype(F32))
        return carry

    lax.fori_loop(0, nkb, bias_body, 0)


def _attn_b(bq, iq, iw, bk, bv, ik2, *, nb, s_q, l_keys, tq, tk, q_pos0, l_valid, topk):
    nq = s_q // tq
    nkb_max = l_keys // tk
    idx_bits = max(1, int(l_keys - 1).bit_length())
    assert l_keys // LANES <= 256, "per-lane key counts must stay exact in bf16"
    group = 2 if nkb_max % 2 == 0 else 1
    kern = functools.partial(_attn_b_kernel, tq=tq, tk=tk, q_pos0=q_pos0, l_valid=l_valid,
                             topk=topk, idx_bits=idx_bits, group=group)
    qmap = lambda b, i: (b * nq + i, 0)
    kmap = lambda b, i: (b, 0)
    return pl.pallas_call(
        kern, grid=(nb, nq),
        in_specs=[pl.BlockSpec((tq, HW), qmap), pl.BlockSpec((tq, HW), qmap),
                  pl.BlockSpec((tq, LANES), qmap),
                  pl.BlockSpec((l_keys, HW), kmap, pipeline_mode=pl.Buffered(1)),
                  pl.BlockSpec((l_keys, HW), kmap, pipeline_mode=pl.Buffered(1)),
                  pl.BlockSpec((l_keys, LANES), kmap, pipeline_mode=pl.Buffered(1))],
        out_specs=pl.BlockSpec((tq, HW), qmap),
        out_shape=jax.ShapeDtypeStruct((nb * s_q, HW), BF16),
        scratch_shapes=[pltpu.VMEM((nkb_max, tk, tq) if tq % LANES == 0 else (nkb_max, tq, tk), I32),
                        pltpu.VMEM((nkb_max, tq, tk), F32),
                        pltpu.VMEM((HEADS, tq, LANES), F32), pltpu.VMEM((HEADS, tq, 2 * LANES), F32)],
        compiler_params=_params(2), name=f"attn_b_q{s_q}")(bq, iq, iw, bk, bv, ik2)


def _attn_c_kernel(*refs, tq, off):
    q_ref = refs[0]
    k_refs = refs[1:1 + C_BAND_BLOCKS]
    v_refs = refs[1 + C_BAND_BLOCKS:1 + 2 * C_BAND_BLOCKS]
    bias_ref = refs[1 + 2 * C_BAND_BLOCKS]
    o_ref = refs[2 + 2 * C_BAND_BLOCKS]
    kc_ref, vc_ref = refs[3 + 2 * C_BAND_BLOCKS:]
    qi = pl.program_id(1)
    for j in range(C_BAND_BLOCKS):
        kc_ref[j * C_KEY_BLOCK:(j + 1) * C_KEY_BLOCK, :] = k_refs[j][...]
        vc_ref[j * C_KEY_BLOCK:(j + 1) * C_KEY_BLOCK, :] = v_refs[j][...]
    col = lax.broadcasted_iota(I32, (1, C_BAND), 1)
    qblk = tq // C_KEY_BLOCK
    col_valid = col >= (off - qi * qblk) * C_KEY_BLOCK
    q = q_ref[...]
    o_even = None
    for h in range(HEADS):
        pr = h // 2
        qm = jnp.where(_head_half_mask((tq, LANES), h), q[:, pr * LANES:(pr + 1) * LANES],
                       jnp.zeros((tq, LANES), BF16))
        s = lax.dot_general(qm, kc_ref[:, pr * LANES:(pr + 1) * LANES], (((1,), (1,)), ((), ())),
                            preferred_element_type=F32) + bias_ref[h]
        s = jnp.where(col_valid, s, NEG_INF)
        sl = [s[:, j * LANES:(j + 1) * LANES] for j in range(C_BAND // LANES)]
        m = jnp.broadcast_to(jnp.max(functools.reduce(jnp.maximum, sl), axis=1, keepdims=True),
                             (tq, LANES))
        ps = [jnp.exp2(x - m) for x in sl]
        l = jnp.broadcast_to(jnp.sum(functools.reduce(jnp.add, ps), axis=1, keepdims=True), (tq, LANES))
        o_h = jnp.dot(jnp.concatenate(ps, axis=1).astype(BF16), vc_ref[:, pr * LANES:(pr + 1) * LANES],
                      preferred_element_type=F32) / l
        if h % 2 == 0:
            o_even = o_h
        else:
            lane = lax.broadcasted_iota(I32, (tq, LANES), 1)
            o_ref[:, pr * LANES:(pr + 1) * LANES] = jnp.where(lane < 64, o_even, o_h).astype(BF16)


def _attn_c(cq, ck, cv, bias, *, nb, s_q, l_keys, tq, off):
    nq = s_q // tq
    nkblk = l_keys // C_KEY_BLOCK
    kern = functools.partial(_attn_c_kernel, tq=tq, off=off)
    qmap = lambda b, i: (b * nq + i, 0)

    qblk = tq // C_KEY_BLOCK

    def kspec(j):
        return pl.BlockSpec((C_KEY_BLOCK, HW),
                            lambda b, i: (b * nkblk + jnp.maximum(i * qblk + j - off, 0), 0))

    kv_specs = [kspec(j) for j in range(C_BAND_BLOCKS)]
    return pl.pallas_call(
        kern, grid=(nb, nq),
        in_specs=[pl.BlockSpec((tq, HW), qmap)] + kv_specs + kv_specs + [_resident(bias)],
        out_specs=pl.BlockSpec((tq, HW), qmap),
        out_shape=jax.ShapeDtypeStruct((nb * s_q, HW), BF16),
        scratch_shapes=[pltpu.VMEM((C_BAND, HW), BF16), pltpu.VMEM((C_BAND, HW), BF16)],
        compiler_params=_params(2), name=f"attn_c_q{s_q}")(
            cq, *([ck] * C_BAND_BLOCKS), *([cv] * C_BAND_BLOCKS), bias)


def _merge_kernel(x_ref, g1_ref, oa_ref, ob_ref, oc_ref, wg_ref, woa_ref, wob_ref, woc_ref, wout_ref, y_ref):
    x = x_ref[...]
    xn = _rms(x, g1_ref[...]).astype(BF16)
    merged = jnp.zeros(x.shape, F32)
    for b, (o_ref, w_ref) in enumerate(((oa_ref, woa_ref), (ob_ref, wob_ref), (oc_ref, woc_ref))):
        gate = jax.nn.sigmoid(jnp.dot(xn, wg_ref[:, b * D_MODEL:(b + 1) * D_MODEL],
                                      preferred_element_type=F32))
        merged = merged + gate * jnp.dot(o_ref[...], w_ref[...], preferred_element_type=F32)
    y_ref[...] = x + jnp.dot(merged.astype(BF16), wout_ref[...], preferred_element_type=F32)


def _mlp_kernel(x_ref, g2_ref, wup_ref, wdn_ref, gf_ref, y_ref, *, ff_chunk, final):
    x = x_ref[...]
    xn = _rms(x, g2_ref[...]).astype(BF16)
    acc = jnp.zeros(x.shape, F32)
    for c in range(D_FF // ff_chunk):
        h = jnp.dot(xn, wup_ref[:, c * ff_chunk:(c + 1) * ff_chunk], preferred_element_type=F32)
        h = jnp.square(jnp.maximum(h, 0.0)).astype(BF16)
        acc = acc + jnp.dot(h, wdn_ref[c * ff_chunk:(c + 1) * ff_chunk, :], preferred_element_type=F32)
    y = x + acc
    if final:
        y = _rms(y, gf_ref[...])
    y_ref[...] = y


def _rope_tables(pos, rot, period, live_lanes):
    half = rot // 2
    n = pos.shape[0]
    inv_freq = ROPE_THETA ** (-jnp.arange(half, dtype=F32) / half)
    ang = pos.astype(F32)[:, None] * inv_freq[None, :]
    cos, sin = jnp.cos(ang), jnp.sin(ang)
    reps = live_lanes // period
    dead = LANES - reps * period
    cos_p = jnp.concatenate([cos, cos, jnp.ones((n, period - rot), F32)], axis=1)
    sin_p = jnp.concatenate([-sin, sin, jnp.zeros((n, period - rot), F32)], axis=1)
    cos_t = jnp.concatenate([cos_p] * reps + [jnp.ones((n, dead), F32)], axis=1)
    sin_t = jnp.concatenate([sin_p] * reps + [jnp.zeros((n, dead), F32)], axis=1)
    return cos_t, sin_t


def _layer_weights(w_in, a_w_uq, a_w_uk, a_w_uv):
    offs = np.cumsum([0, A_Q_LORA, A_KV_LORA, A_ROPE, HW, HW, HW, HW, IDX_DIM, HEADS, HW, HW, HW,
                      3 * D_MODEL])
    seg = lambda i: w_in[:, offs[i]:offs[i + 1]]
    zeros = lambda n: jnp.zeros((D_MODEL, n), w_in.dtype)
    w_a = jnp.concatenate([seg(0), seg(1), seg(2), zeros(LANES - A_ROPE)], axis=1)
    w_b = jnp.concatenate([seg(3), seg(4), seg(5), seg(6), seg(7), zeros(LANES - IDX_DIM),
                           seg(8), zeros(LANES - HEADS)], axis=1)
    w_c = jnp.concatenate([seg(9), seg(10), seg(11)], axis=1)
    w_g = seg(12)
    uq = a_w_uq.reshape(A_Q_LORA, HEADS, A_NOPE + A_ROPE)
    uq_nope = uq[:, :, :A_NOPE].reshape(A_Q_LORA, HW)
    uq_rope = jnp.pad(uq[:, :, A_NOPE:], ((0, 0), (0, 0), (0, LANES - A_ROPE))).reshape(A_Q_LORA, HEADS * LANES)
    w_uq = jnp.concatenate([uq_nope, uq_rope], axis=1)
    uk = a_w_uk.reshape(A_KV_LORA, HEADS, A_NOPE)
    ukt = jnp.transpose(uk, (1, 2, 0))
    ukt_pad = jnp.stack([jnp.pad(ukt[h], ((0, 64), (0, 0)) if h % 2 == 0 else ((64, 0), (0, 0)))
                         for h in range(HEADS)], axis=0)
    uv = a_w_uv.reshape(A_KV_LORA, HEADS, A_VDIM)
    uv_pad = jnp.stack([jnp.pad(uv[:, h], ((0, 0), (h * A_VDIM, HW - (h + 1) * A_VDIM)))
                        for h in range(HEADS)], axis=0)
    bf = lambda a: a.astype(BF16)
    return bf(w_a), bf(w_b), bf(w_c), bf(w_g), bf(w_uq), bf(ukt_pad), bf(uv_pad)


def _band_bias(rel_bias, q_pos, k_pos, k_live):
    q_c, k_c = q_pos // CHUNK, k_pos // CHUNK
    vis = (k_c[None, :] <= q_c[:, None]) & (k_c[None, :] >= q_c[:, None] - C_LEFT_CHUNKS) & k_live[None, :]
    nq, nk = len(q_pos), len(k_pos)
    diag = (q_pos[0] - k_pos[0]) + (nq - 1) - np.arange(nq + nk - 1)
    ext = rel_bias[:, np.clip(diag, -C_REL_CLIP, C_REL_CLIP) + C_REL_CLIP].astype(F32) * LOG2E
    bias = jnp.stack([ext[:, nq - 1 - i:nq - 1 - i + nk] for i in range(nq)], axis=1)
    return jnp.where(vis[None], bias, NEG_INF)


def _pad_rows(a, n):
    return jnp.pad(a, ((0, 0), (0, n - a.shape[1]), (0, 0)))


def _mixers(x2d, lw, *, nb, s_q, q_pos0, past, tm, tq_a, tq_b, tq_c, tk_a, tk_b, tables, c_rel_bias):
    (g1, gq, gkv, w_a, w_b, w_c, w_uq, ukt_pad, uv_pad) = lw
    cos_a, sin_a, cos_b, sin_b, cos_k, sin_k = tables
    t = nb * s_q
    res = _resident
    qa, ka, ckv32, kr32 = _row_call(
        _prep_a_kernel, x2d, tm, (g1, w_a, gq, gkv, w_uq, ukt_pad, cos_a, sin_a),
        [res(g1), res(w_a), res(gq), res(gkv), res(w_uq), res(ukt_pad), _table_spec(cos_a, tm),
         _table_spec(sin_a, tm)],
        [(HEADS * 256, BF16), (256, BF16), (A_KV_LORA, F32), (A_ROPE, F32)], "prep_a")
    bq, bk, bv, iq, ik2, iw, bk32, bv32, ik32 = _row_call(
        _prep_b_kernel, x2d, tm, (g1, w_b, cos_b, sin_b, cos_k, sin_k),
        [res(g1), res(w_b), _table_spec(cos_b, tm), _table_spec(sin_b, tm), _table_spec(cos_k, tm),
         _table_spec(sin_k, tm)],
        [(HW, BF16), (HW, BF16), (HW, BF16), (HW, BF16), (LANES, BF16), (LANES, F32),
         (HW, F32), (HW, F32), (IDX_DIM, F32)], "prep_b")
    cq, ck, cv = _row_call(_prep_c_kernel, x2d, tm, (g1, w_c), [res(g1), res(w_c)],
                           [(HW, BF16), (HW, BF16), (HW, BF16)], "prep_c")

    if past is None:
        l_valid = s_q
        l_keys = s_q
        keys_a, keys_bk, keys_bv, keys_ik = ka, bk, bv, ik2
        keys_ck, keys_cv = ck, cv
        lc_keys = s_q
        c_off = C_REACH // C_KEY_BLOCK
        assert c_off + tq_c // C_KEY_BLOCK == C_BAND_BLOCKS
        qp = np.arange(tq_c)
        kp = np.arange(C_BAND) - C_REACH
        bias = _band_bias(c_rel_bias, qp, kp, np.ones(C_BAND, bool))
        keep = min(C_REACH, s_q)
        x_tail = x2d.reshape(nb, s_q, D_MODEL)[:, s_q - keep:].reshape(nb * keep, D_MODEL)
        w_ckv = w_c[:, HW:]
        ck32, cv32 = _row_call(_prep_c32_kernel, x_tail, min(tm, nb * keep), (g1, w_ckv),
                               [res(g1), res(w_ckv)], [(HW, F32), (HW, F32)], "prep_c32")
        ck32 = ck32.reshape(nb, keep, HEADS, C_HEAD_DIM)
        cv32 = cv32.reshape(nb, keep, HEADS, C_HEAD_DIM)
    else:
        p_ckv, p_kr, p_bk, p_bv, p_ik, p_ck, p_cv = past
        past_len = p_ckv.shape[1]
        l_valid = past_len + s_q
        l_keys = -(-l_valid // LANES) * LANES
        r3 = lambda a: a.reshape(nb, s_q, a.shape[-1])
        cat = lambda old, new: _pad_rows(jnp.concatenate([old.astype(BF16), r3(new)], axis=1), l_keys)
        flat = lambda a: a.reshape(nb * a.shape[1], a.shape[2])
        old_a = jnp.concatenate([p_ckv, p_kr, jnp.zeros((nb, past_len, 256 - A_KV_LORA - A_ROPE), F32)], axis=-1)
        keys_a = flat(cat(old_a, ka))
        keys_bk = flat(cat(p_bk.reshape(nb, past_len, HW), bk))
        keys_bv = flat(cat(p_bv.reshape(nb, past_len, HW), bv))
        keys_ik = flat(cat(jnp.concatenate([p_ik, p_ik], axis=-1), ik2))
        w_c_len = p_ck.shape[1]
        lc_valid = w_c_len + s_q
        lc_keys = C_BAND
        catc = lambda old, new: _pad_rows(jnp.concatenate([old.astype(BF16), r3(new)], axis=1), lc_keys)
        keys_ck = flat(catc(p_ck.reshape(nb, w_c_len, HW), ck))
        keys_cv = flat(catc(p_cv.reshape(nb, w_c_len, HW), cv))
        c_off = 0
        qp = np.arange(q_pos0, q_pos0 + s_q)
        kp = np.arange(past_len - w_c_len, past_len - w_c_len + lc_keys)
        bias = _band_bias(c_rel_bias, qp, kp, np.arange(lc_keys) < lc_valid)
        ck32, cv32 = _row_call(_prep_c32_kernel, x2d, tm, (g1, w_c[:, HW:]),
                               [res(g1), res(w_c[:, HW:])], [(HW, F32), (HW, F32)], "prep_c32")
        ck32 = ck32.reshape(nb, s_q, HEADS, C_HEAD_DIM)
        cv32 = cv32.reshape(nb, s_q, HEADS, C_HEAD_DIM)

    topk = min(B_TOPK_MAX, l_valid // 4)
    tk_a = min(tk_a, l_keys)
    tk_b = min(tk_b, l_keys)
    o_a = _attn_a(qa, keys_a, uv_pad, nb=nb, s_q=s_q, l_keys=l_keys, tq=tq_a, tk=tk_a,
                  q_pos0=q_pos0, l_valid=l_valid)
    o_b = _attn_b(bq, iq, iw, keys_bk, keys_bv, keys_ik, nb=nb, s_q=s_q, l_keys=l_keys, tq=tq_b,
                  tk=tk_b, q_pos0=q_pos0, l_valid=l_valid, topk=topk)
    o_c = _attn_c(cq, keys_ck, keys_cv, bias, nb=nb, s_q=s_q, l_keys=lc_keys, tq=tq_c, off=c_off)
    new_rows = (ckv32.reshape(nb, s_q, A_KV_LORA), kr32.reshape(nb, s_q, A_ROPE),
                bk32.reshape(nb, s_q, HEADS, B_HEAD_DIM), bv32.reshape(nb, s_q, HEADS, B_HEAD_DIM),
                ik32.reshape(nb, s_q, IDX_DIM), ck32, cv32)
    return o_a, o_b, o_c, new_rows


def _layer(x2d, lw_mix, lw_rest, gf, *, final, tm, **kw):
    o_a, o_b, o_c, new_rows = _mixers(x2d, lw_mix, tm=tm, **kw)
    g1 = lw_mix[0]
    (w_g, w_oa, w_ob, w_oc, w_out, g2, w_up, w_dn) = lw_rest
    res = _resident
    row = lambda w: pl.BlockSpec((tm, w), lambda i: (i, 0))
    (x1,) = _row_call(_merge_kernel, x2d, tm, (g1, o_a, o_b, o_c, w_g, w_oa, w_ob, w_oc, w_out),
                      [res(g1), row(HW), row(HW), row(HW), res(w_g), res(w_oa), res(w_ob), res(w_oc),
                       res(w_out)], [(D_MODEL, F32)], "merge")
    (x2,) = _row_call(functools.partial(_mlp_kernel, ff_chunk=1024, final=final), x1, tm,
                      (g2, w_up, w_dn, gf), [res(g2), res(w_up), res(w_dn), res(gf)], [(D_MODEL, F32)],
                      "mlp_final" if final else "mlp")
    return x2, new_rows


def _tiles(nb, s_q, prompt):
    if prompt:
        tm = 512 if (nb * s_q) % 512 == 0 else 256
        return dict(tm=tm, tq_a=128, tq_b=256, tq_c=256, tk_a=512, tk_b=512)
    whole = 1 << 30
    return dict(tm=nb * s_q, tq_a=s_q, tq_b=s_q, tq_c=s_q, tk_a=whole, tk_b=whole)


def kernel(x_prompt, x_sample, cache_a_ckv, cache_a_krope, cache_b_k, cache_b_v, cache_b_idx_k, cache_c_k, cache_c_v, norm1, w_in, a_q_norm, a_kv_norm, a_w_uq, a_w_uk, a_w_uv, c_rel_bias, w_oa, w_ob, w_oc, w_out, norm2, w_up, w_down, final_norm):
    depth = w_in.shape[0]
    nb_p, s_p, _ = x_prompt.shape
    nb_s, s_s, _ = x_sample.shape
    past_len = cache_a_ckv.shape[2]
    assert s_p % 256 == 0 and CHUNK % s_s == 0 and past_len % CHUNK == 0
    tiles_p, tiles_s = _tiles(nb_p, s_p, prompt=True), _tiles(nb_s, s_s, prompt=False)

    pos_p = jnp.arange(s_p, dtype=jnp.int32)
    pos_s = jnp.tile(jnp.arange(past_len, past_len + s_s, dtype=jnp.int32), nb_s)

    def tables(pos):
        return (_rope_tables(pos, A_ROPE, A_ROPE, A_ROPE) + _rope_tables(pos, B_ROT, B_HEAD_DIM, LANES)
                + _rope_tables(pos, IDX_ROT, IDX_DIM, IDX_DIM))

    tab_p, tab_s = tables(pos_p), tables(pos_s)
    gf = final_norm.reshape(1, D_MODEL)
    bf = lambda a: a.astype(BF16)

    xp = x_prompt.reshape(nb_p * s_p, D_MODEL)
    xs = x_sample.reshape(nb_s * s_s, D_MODEL)
    rows_p, rows_s = [], []
    for l in range(depth):
        w_a, w_b, w_c, w_g, w_uq, ukt_pad, uv_pad = _layer_weights(w_in[l], a_w_uq[l], a_w_uk[l], a_w_uv[l])
        lw_mix = (norm1[l].reshape(1, -1), a_q_norm[l].reshape(1, -1), a_kv_norm[l].reshape(1, -1),
                  w_a, w_b, w_c, w_uq, ukt_pad, uv_pad)
        lw_rest = (w_g, bf(w_oa[l]), bf(w_ob[l]), bf(w_oc[l]), bf(w_out[l]), norm2[l].reshape(1, -1),
                   bf(w_up[l]), bf(w_down[l]))
        final = l == depth - 1
        past = (cache_a_ckv[l], cache_a_krope[l], cache_b_k[l], cache_b_v[l], cache_b_idx_k[l],
                cache_c_k[l], cache_c_v[l])
        xp, new_p = _layer(xp, lw_mix, lw_rest, gf, final=final, nb=nb_p, s_q=s_p, q_pos0=0,
                           past=None, tables=tab_p, c_rel_bias=c_rel_bias[l], **tiles_p)
        xs, new_s = _layer(xs, lw_mix, lw_rest, gf, final=final, nb=nb_s, s_q=s_s, q_pos0=past_len,
                           past=past, tables=tab_s, c_rel_bias=c_rel_bias[l], **tiles_s)
        rows_p.append(new_p)
        rows_s.append(new_s)

    y_prompt = xp.reshape(nb_p, s_p, D_MODEL)
    y_sample = xs.reshape(nb_s, s_s, D_MODEL)
    stack = lambda rows, i: jnp.stack([r[i] for r in rows], axis=0)
    outs = [y_prompt, y_sample]
    for i in range(7):
        outs += [stack(rows_p, i), stack(rows_s, i)]
    return tuple(outs)
```
